```python
import jax, jax.numpy as jnp
from jax import lax
import numpy as np

D_MODEL = 1024
BATCH = 2
SEQ = 8192
DEPTH = 1

HG_HEADS = 8
HG_DK = 128
HG_DV = 128
HG_CHUNK = 64
RET_HEADS = 4
RET_DK = 256
RET_DV = 512
RET_CHUNK = 128
ROPE_BASE = 10000.0
PEER_HEADS = 8
PEER_NKEYS = 128
PEER_N_EXPERTS = PEER_NKEYS * PEER_NKEYS
PEER_QDIM = 256
PEER_HALF = PEER_QDIM // 2
PEER_TOPK = 16
PEER_BLOCK = 128
NORM_EPS = 1e-6

IN_SIZES = (HG_HEADS * HG_DK, HG_HEADS * HG_DK, HG_HEADS * HG_DV, HG_HEADS * HG_DV,
            RET_HEADS * RET_DK, RET_HEADS * RET_DK, RET_HEADS * RET_DV, RET_HEADS * RET_DV,
            D_MODEL, D_MODEL)
IN_WIDTH = (2 * HG_HEADS * HG_DK + 2 * HG_HEADS * HG_DV + 2 * RET_HEADS * RET_DK
            + 2 * RET_HEADS * RET_DV + 2 * D_MODEL)

kernel_name = "hybrid_hgrn2_retnet_peer"


def rmsnorm(x, w):
    xf = x.astype(jnp.float32)
    y = xf * lax.rsqrt(jnp.mean(xf * xf, axis=-1, keepdims=True) + NORM_EPS)
    return (y * w.astype(jnp.float32)).astype(x.dtype)


def head_rms(x):
    return x * lax.rsqrt(jnp.mean(x * x, axis=-1, keepdims=True) + NORM_EPS)


def split_cols(t, sizes):
    outs, start = [], 0
    for n in sizes:
        outs.append(t[..., start:start + n])
        start += n
    return outs


def to_chunks(t, chunk):
    B, S, H, d = t.shape
    return t.reshape(B, S // chunk, chunk, H, d).transpose(1, 0, 3, 2, 4)


def from_chunks(t):
    nc, B, H, C, d = t.shape
    return t.transpose(1, 0, 3, 2, 4).reshape(B, nc * C, H, d)


def hgrn2_mix(q, f_logit, i, lb):
    B, S, _ = q.shape
    f = lb + (1.0 - lb) * jax.nn.sigmoid(f_logit)
    k = 1.0 - f
    logf = jnp.log(f)
    qc = to_chunks((q * HG_DK ** -0.5).reshape(B, S, HG_HEADS, HG_DK), HG_CHUNK)
    kc = to_chunks(k.reshape(B, S, HG_HEADS, HG_DK), HG_CHUNK)
    gc = to_chunks(logf.reshape(B, S, HG_HEADS, HG_DK), HG_CHUNK)
    vc = to_chunks(i.reshape(B, S, HG_HEADS, HG_DV), HG_CHUNK)
    causal = jnp.tril(jnp.ones((HG_CHUNK, HG_CHUNK), dtype=bool))

    def step(state, inp):
        qb, kb, gb, vb = inp
        b = jnp.cumsum(gb, axis=2)
        diff = b[:, :, :, None, :] - b[:, :, None, :, :]
        decay = jnp.exp(jnp.where(causal[:, :, None], diff, -jnp.inf))
        a = jnp.einsum('bhtd,bhsd,bhtsd->bhts', qb, kb, decay)
        o = (jnp.einsum('bhts,bhsv->bhtv', a, vb)
             + jnp.einsum('bhtd,bhdv->bhtv', qb * jnp.exp(b), state))
        b_last = b[:, :, -1:, :]
        state = (jnp.exp(b_last[:, :, 0, :])[..., None] * state
                 + jnp.einsum('bhsd,bhsv->bhdv', kb * jnp.exp(b_last - b), vb))
        return state, o

    s0 = jnp.zeros((B, HG_HEADS, HG_DK, HG_DV), jnp.float32)
    _, o = lax.scan(step, s0, (qc, kc, gc, vc))
    return from_chunks(o)


def rotary(x, pos):
    half = x.shape[-1] // 2
    inv = ROPE_BASE ** (-jnp.arange(half, dtype=jnp.float32) / half)
    ang = pos[:, None] * inv[None, :]
    cos = jnp.cos(ang)[None, :, None, :]
    sin = jnp.sin(ang)[None, :, None, :]
    x1, x2 = x[..., :half], x[..., half:]
    return jnp.concatenate([x1 * cos - x2 * sin, x1 * sin + x2 * cos], axis=-1)


def retention_mix(q, k, v):
    C = RET_CHUNK
    log_g = jnp.log(1.0 - jnp.exp2(-5.0 - jnp.arange(RET_HEADS, dtype=jnp.float32)))
    idx = jnp.arange(C, dtype=jnp.float32)
    rel = idx[:, None] - idx[None, :]
    dmask = jnp.where(rel >= 0, jnp.exp(log_g[:, None, None] * jnp.maximum(rel, 0.0)), 0.0)
    q_decay = jnp.exp(log_g[:, None] * (idx + 1.0))[None, :, :, None]
    k_decay = jnp.exp(log_g[:, None] * (C - 1.0 - idx))[None, :, :, None]
    chunk_decay = jnp.exp(log_g * C)[None, :, None, None]
    qc, kc, vc = to_chunks(q, C), to_chunks(k, C), to_chunks(v, C)

    def step(state, inp):
        qb, kb, vb = inp
        a = jnp.einsum('bhtd,bhsd->bhts', qb, kb) * dmask[None]
        o = (jnp.einsum('bhts,bhsv->bhtv', a, vb)
             + jnp.einsum('bhtd,bhdv->bhtv', qb * q_decay, state))
        state = chunk_decay * state + jnp.einsum('bhsd,bhsv->bhdv', kb * k_decay, vb)
        return state, o

    B = q.shape[0]
    s0 = jnp.zeros((B, RET_HEADS, RET_DK, RET_DV), jnp.float32)
    _, o = lax.scan(step, s0, (qc, kc, vc))
    return from_chunks(o)


def peer_ffn(h, w_q, sub_keys, expert_u, expert_v):
    B, S, D = h.shape
    T = B * S
    ht = h.reshape(T, D)
    q = (ht @ w_q).astype(jnp.float32).reshape(T, PEER_HEADS, 2, PEER_HALF)
    s = jnp.einsum('thpd,hpkd->thpk', q, sub_keys.astype(jnp.float32))
    top_s, top_i = lax.top_k(s, PEER_TOPK)
    cand = top_s[:, :, 0, :, None] + top_s[:, :, 1, None, :]
    cand_idx = top_i[:, :, 0, :, None] * PEER_NKEYS + top_i[:, :, 1, None, :]
    best_s, best_c = lax.top_k(cand.reshape(T, PEER_HEADS, PEER_TOPK * PEER_TOPK), PEER_TOPK)
    expert_idx = jnp.take_along_axis(
        cand_idx.reshape(T, PEER_HEADS, PEER_TOPK * PEER_TOPK), best_c, axis=-1)
    gates = jax.nn.softmax(best_s, axis=-1)
    nb = T // PEER_BLOCK

    def block_fn(args):
        xb, ib, gb = args
        u = jnp.take(expert_u, ib, axis=0)
        act = jax.nn.gelu(jnp.einsum('td,thkd->thk', xb, u).astype(jnp.float32))
        w = (gb * act).astype(xb.dtype)
        v = jnp.take(expert_v, ib, axis=0)
        return jnp.einsum('thk,thkd->td', w, v)

    out = lax.map(block_fn, (ht.reshape(nb, PEER_BLOCK, D),
                             expert_idx.reshape(nb, PEER_BLOCK, PEER_HEADS, PEER_TOPK),
                             gates.reshape(nb, PEER_BLOCK, PEER_HEADS, PEER_TOPK)))
    return out.reshape(B, S, D)


def setup_inputs(seed: int = 0) -> dict:
    key = jax.random.key(seed)
    ks = jax.random.split(key, 14)
    f32 = jnp.float32

    def nrm(k, shape, scale):
        return jax.random.normal(k, shape, f32) * scale

    return {
        "x": nrm(ks[0], (BATCH, SEQ, D_MODEL), 1.0),
        "norm_mix_w": 1.0 + nrm(ks[1], (DEPTH, D_MODEL), 0.02),
        "w_in": nrm(ks[2], (DEPTH, D_MODEL, IN_WIDTH), D_MODEL ** -0.5),
        "hg_lower_bounds": nrm(ks[3], (DEPTH + 1, HG_HEADS * HG_DK), 0.1),
        "hg_norm_w": 1.0 + nrm(ks[4], (DEPTH, HG_HEADS * HG_DV), 0.02),
        "w_branch_hg": nrm(ks[5], (DEPTH, HG_HEADS * HG_DV, D_MODEL), (HG_HEADS * HG_DV) ** -0.5),
        "w_branch_ret": nrm(ks[6], (DEPTH, RET_HEADS * RET_DV, D_MODEL), (RET_HEADS * RET_DV) ** -0.5),
        "w_out": nrm(ks[7], (DEPTH, D_MODEL, D_MODEL), D_MODEL ** -0.5),
        "norm_ffn_w": 1.0 + nrm(ks[8], (DEPTH, D_MODEL), 0.02),
        "peer_w_q": nrm(ks[9], (DEPTH, D_MODEL, PEER_HEADS * PEER_QDIM), D_MODEL ** -0.5),
        "peer_sub_keys": nrm(ks[10], (DEPTH, PEER_HEADS, 2, PEER_NKEYS, PEER_HALF), PEER_HALF ** -0.5),
        "expert_u": nrm(ks[11], (DEPTH, PEER_N_EXPERTS, D_MODEL), D_MODEL ** -0.5),
        "expert_v": nrm(ks[12], (DEPTH, PEER_N_EXPERTS, D_MODEL), PEER_HEADS ** -0.5),
        "final_norm_w": 1.0 + nrm(ks[13], (D_MODEL,), 0.02),
    }


def reference(x, norm_mix_w, w_in, hg_lower_bounds, hg_norm_w, w_branch_hg, w_branch_ret,
              w_out, norm_ffn_w, peer_w_q, peer_sub_keys, expert_u, expert_v, final_norm_w):
    B, S, _ = x.shape
    f32 = jnp.float32
    lbs = jnp.cumsum(jax.nn.softmax(hg_lower_bounds.astype(f32), axis=0), axis=0)
    pos = jnp.arange(S, dtype=f32)
    for l in range(DEPTH):
        h = rmsnorm(x, norm_mix_w[l])
        proj = h @ w_in[l]
        hq, hf, hi, hg, rq, rk, rv, rg, gate_a, gate_b = split_cols(proj, IN_SIZES)

        o_hg = hgrn2_mix(hq.astype(f32), hf.astype(f32), hi.astype(f32), lbs[l])
        o_hg = (head_rms(o_hg).reshape(B, S, HG_HEADS * HG_DV) * hg_norm_w[l].astype(f32)
                * jax.nn.silu(hg.astype(f32)))

        q = rotary(rq.astype(f32).reshape(B, S, RET_HEADS, RET_DK), pos)
        k = rotary(rk.astype(f32).reshape(B, S, RET_HEADS, RET_DK), pos) * RET_DK ** -0.5
        v = rv.astype(f32).reshape(B, S, RET_HEADS, RET_DV)
        o_ret = retention_mix(q, k, v)
        o_ret = head_rms(o_ret).reshape(B, S, RET_HEADS * RET_DV) * jax.nn.silu(rg.astype(f32))

        y_hg = o_hg.astype(x.dtype) @ w_branch_hg[l]
        y_ret = o_ret.astype(x.dtype) @ w_branch_ret[l]
        merged = jax.nn.sigmoid(gate_a) * y_hg + jax.nn.sigmoid(gate_b) * y_ret
        x = x + merged @ w_out[l]

        h = rmsnorm(x, norm_ffn_w[l])
        x = x + peer_ffn(h, peer_w_q[l], peer_sub_keys[l], expert_u[l], expert_v[l])
    return rmsnorm(x, final_norm_w)
```

```python
import functools

import jax
import jax.numpy as jnp
from jax import lax
from jax.experimental import pallas as pl
from jax.experimental.pallas import tpu as pltpu

F32 = jnp.float32
BF16 = jnp.bfloat16

D_MODEL = 1024
HG_HEADS = 8
HG_DK = 128
HG_DV = 128
RET_HEADS = 4
RET_DK = 256
RET_DV = 512
RET_CHUNK = 128
ROPE_BASE = 10000.0
PEER_HEADS = 8
PEER_NKEYS = 128
PEER_HALF = 128
PEER_TOPK = 16
NORM_EPS = 1e-6
IN_WIDTH = 12288

VMEM_LIMIT_BYTES = 56 * 1024 * 1024

HG_CHUNK = 64
HG_SUB = 16
NOT_SELECTED_RANK = 99.0


def _sigmoid(x):
    return 1.0 / (1.0 + jnp.exp(-x))


def _dot(a, b):
    return jnp.dot(a, b, preferred_element_type=F32)


def _dot_nt(a, b):
    return lax.dot_general(a, b, (((1,), (1,)), ((), ())), preferred_element_type=F32)


def _dot_tn(a, b):
    return lax.dot_general(a, b, (((0,), (0,)), ((), ())), preferred_element_type=F32)


def _params(sem):
    return pltpu.CompilerParams(dimension_semantics=sem, vmem_limit_bytes=VMEM_LIMIT_BYTES)


def _inproj_body(x_ref, nw_ref, w_ref, o_ref, h_scr):
    @pl.when(pl.program_id(1) == 0)
    def _():
        x = x_ref[...]
        ms = jnp.mean(x * x, axis=-1, keepdims=True)
        h_scr[...] = (x * lax.rsqrt(ms + NORM_EPS) * nw_ref[...]).astype(BF16)

    o_ref[...] = _dot(h_scr[...], w_ref[...]).astype(o_ref.dtype)


def _inproj(x2, norm_w, w_bf, out_dtype, tm=1024, tn=1536):
    t, d = x2.shape
    n = w_bf.shape[1]
    return pl.pallas_call(
        _inproj_body,
        grid=(t // tm, n // tn),
        in_specs=[
            pl.BlockSpec((tm, d), lambda i, j: (i, 0)),
            pl.BlockSpec((1, d), lambda i, j: (0, 0)),
            pl.BlockSpec((d, tn), lambda i, j: (0, j)),
        ],
        out_specs=pl.BlockSpec((tm, tn), lambda i, j: (i, j)),
        out_shape=jax.ShapeDtypeStruct((t, n), out_dtype),
        scratch_shapes=[pltpu.VMEM((tm, d), BF16)],
        compiler_params=_params(("parallel", "arbitrary")),
        name="inproj",
    )(x2, norm_w, w_bf)


def _rows4(vals, n):
    return jnp.concatenate([jnp.broadcast_to(v, (HG_SUB, n)) for v in vals], axis=0)


def _hgrn2_body(q_ref, f_ref, i_ref, g_ref, lbp_ref, nw_ref, o_ref, st_ref, *, n_chunks):
    c, sub, dk = HG_CHUNK, HG_SUB, HG_DK

    @pl.when(pl.program_id(2) == 0)
    def _():
        st_ref[...] = jnp.zeros_like(st_ref)

    lbp = lbp_ref[...]
    mx = jnp.max(lbp, axis=0, keepdims=True)
    el = jnp.exp(lbp - mx)
    lb = el[0:1, :] / (el[0:1, :] + el[1:2, :])
    nw = nw_ref[...]

    rin = lax.broadcasted_iota(jnp.int32, (c, dk), 0) & (sub - 1)
    r16 = lax.broadcasted_iota(jnp.int32, (sub, dk), 0)
    rb = lax.broadcasted_iota(jnp.int32, (c, c), 0) >> 4
    cb = lax.broadcasted_iota(jnp.int32, (c, c), 1) >> 4
    m16 = ((rb & 1) == 1) & (cb == rb - 1)
    m32 = (rb >= 2) & (cb < 2)
    ones_b = jnp.ones((dk, dk), BF16)
    zero = jnp.zeros((1, dk), F32)
    one = jnp.ones((1, dk), F32)

    def chunk(ci, carry):
        sl = pl.ds(pl.multiple_of(ci * c, c), c)
        q = q_ref[sl, :].astype(F32) * (HG_DK ** -0.5)
        fl = f_ref[sl, :].astype(F32)
        v = i_ref[sl, :].astype(F32)
        f = lb + (1.0 - lb) * _sigmoid(fl)
        k = 1.0 - f
        lf = jnp.log(f)

        bl = lf
        for sh in (1, 2, 4, 8):
            bl = bl + jnp.where(rin >= sh, pltpu.roll(bl, sh, 0), 0.0)
        g = [bl[sub * j + sub - 1:sub * j + sub, :] for j in range(4)]
        eg = [jnp.exp(x) for x in g]

        q16 = q * jnp.exp(bl)
        k16 = k * jnp.exp(_rows4(g, dk) - bl)
        q32 = q16 * _rows4([one, one, one, eg[2]], dk)
        k32 = k16 * _rows4([eg[1], one, one, one], dk)
        qin = q16 * _rows4([one, eg[0], eg[0] * eg[1], eg[0] * eg[1] * eg[2]], dk)
        kst = k16 * _rows4([eg[1] * eg[2] * eg[3], eg[2] * eg[3], eg[3], one], dk)

        s16 = _dot_nt(q16.astype(BF16), k16.astype(BF16))
        s32 = _dot_nt(q32.astype(BF16), k32.astype(BF16))
        a = jnp.where(m16, s16, 0.0) + jnp.where(m32, s32, 0.0)
        vb = v.astype(BF16)
        st = st_ref[...]
        o = _dot(a.astype(BF16), vb) + _dot_nt(qin.astype(BF16), st.astype(BF16))

        od = []
        for j in range(4):
            lo = sub * j
            blj, qj, kj, vj = bl[lo:lo + sub], q[lo:lo + sub], k[lo:lo + sub], v[lo:lo + sub]
            ps = []
            for s in range(sub):
                e = jnp.exp(jnp.where(r16 >= s, blj - blj[s:s + 1, :], -jnp.inf))
                ps.append(qj * kj[s:s + 1, :] * e)
            p = jnp.concatenate(ps, axis=0).astype(BF16)
            ar = _dot(p, ones_b)
            acc = ar[0:sub] * vj[0:1, :]
            for s in range(1, sub):
                acc = acc + ar[s * sub:(s + 1) * sub] * vj[s:s + 1, :]
            od.append(acc)
        o = o + jnp.concatenate(od, axis=0)

        st_ref[...] = st * (eg[0] * eg[1] * eg[2] * eg[3]) + _dot_tn(vb, kst.astype(BF16))

        ms = jnp.mean(o * o, axis=-1, keepdims=True)
        gt = g_ref[sl, :].astype(F32)
        o_ref[sl, :] = (o * lax.rsqrt(ms + NORM_EPS) * nw * (gt * _sigmoid(gt))).astype(o_ref.dtype)
        return carry

    lax.fori_loop(0, n_chunks, chunk, 0)


def _hgrn2(proj3, lb_logits, norm_w, ts=512):
    b, s, _ = proj3.shape
    nh = HG_HEADS

    def col(off):
        return pl.BlockSpec((None, ts, HG_DK), lambda bi, h, si, off=off: (bi, si, off + h))

    return pl.pallas_call(
        functools.partial(_hgrn2_body, n_chunks=ts // HG_CHUNK),
        grid=(b, nh, s // ts),
        in_specs=[
            col(0), col(nh), col(2 * nh), col(3 * nh),
            pl.BlockSpec((2, HG_DK), lambda bi, h, si: (0, h)),
            pl.BlockSpec((1, HG_DV), lambda bi, h, si: (0, h)),
        ],
        out_specs=pl.BlockSpec((None, ts, HG_DV), lambda bi, h, si: (bi, si, h)),
        out_shape=jax.ShapeDtypeStruct((b, s, nh * HG_DV), BF16),
        scratch_shapes=[pltpu.VMEM((HG_DV, HG_DK), F32)],
        compiler_params=_params(("parallel", "parallel", "arbitrary")),
        name="hgrn2",
    )(proj3, proj3, proj3, proj3, lb_logits, norm_w)


def _ret_body(q_ref, k_ref, v_ref, g_ref, cos_ref, sin_ref, lg_ref, o_ref, st_ref, *, n_chunks):
    c, half = RET_CHUNK, RET_DK // 2

    @pl.when(pl.program_id(2) == 0)
    def _():
        st_ref[...] = jnp.zeros_like(st_ref)

    lgw = lg_ref[...]
    lg = lgw[:, :half]
    ri = lax.broadcasted_iota(jnp.int32, (c, half), 0).astype(F32)
    ci = lax.broadcasted_iota(jnp.int32, (c, half), 1).astype(F32)
    rel = ri - ci
    dmask = jnp.where(rel >= 0, jnp.exp(lg * jnp.maximum(rel, 0.0)), 0.0)
    qdec = jnp.exp(lg * (ri + 1.0))
    kdec = jnp.exp(lg * (c - 1.0 - ri))
    cdec = jnp.exp(lgw * float(c))

    def rot(x_ref, sl, cs, sn):
        x1 = x_ref[sl, :half].astype(F32)
        x2 = x_ref[sl, half:].astype(F32)
        return x1 * cs - x2 * sn, x1 * sn + x2 * cs

    def cat(x1, x2):
        return jnp.concatenate([x1, x2], axis=1).astype(BF16)

    def chunk(ci_, carry):
        sl = pl.ds(pl.multiple_of(ci_ * c, c), c)
        cs = cos_ref[sl, :]
        sn = sin_ref[sl, :]
        q1, q2 = rot(q_ref, sl, cs, sn)
        k1, k2 = rot(k_ref, sl, cs, sn)
        k1 = k1 * (RET_DK ** -0.5)
        k2 = k2 * (RET_DK ** -0.5)
        vb = v_ref[sl, :].astype(BF16)
        st = st_ref[...]

        a = _dot_nt(cat(q1, q2), cat(k1, k2)) * dmask
        o = _dot(a.astype(BF16), vb) + _dot(cat(q1 * qdec, q2 * qdec), st.astype(BF16))
        st_ref[...] = cdec * st + _dot_tn(cat(k1 * kdec, k2 * kdec), vb)

        ms = jnp.mean(o * o, axis=-1, keepdims=True)
        gt = g_ref[sl, :].astype(F32)
        o_ref[sl, :] = (o * lax.rsqrt(ms + NORM_EPS) * (gt * _sigmoid(gt))).astype(o_ref.dtype)
        return carry

    lax.fori_loop(0, n_chunks, chunk, 0)


def _retention(proj3, cos_t, sin_t, log_gamma, ts=512):
    b, s, _ = proj3.shape
    nh = RET_HEADS
    q_off = 4 * HG_HEADS * HG_DK
    qb, kb = q_off // RET_DK, q_off // RET_DK + nh
    vb, gb = (q_off + 2 * nh * RET_DK) // RET_DV, (q_off + 2 * nh * RET_DK) // RET_DV + nh

    def col(w, off):
        return pl.BlockSpec((None, ts, w), lambda bi, h, si, off=off: (bi, si, off + h))

    return pl.pallas_call(
        functools.partial(_ret_body, n_chunks=ts // RET_CHUNK),
        grid=(b, nh, s // ts),
        in_specs=[
            col(RET_DK, qb), col(RET_DK, kb), col(RET_DV, vb), col(RET_DV, gb),
            pl.BlockSpec((ts, RET_DK // 2), lambda bi, h, si: (si, 0)),
            pl.BlockSpec((ts, RET_DK // 2), lambda bi, h, si: (si, 0)),
            pl.BlockSpec((None, 1, RET_DV), lambda bi, h, si: (h, 0, 0)),
        ],
        out_specs=pl.BlockSpec((None, ts, RET_DV), lambda bi, h, si: (bi, si, h)),
        out_shape=jax.ShapeDtypeStruct((b, s, nh * RET_DV), BF16),
        scratch_shapes=[pltpu.VMEM((RET_DK, RET_DV), F32)],
        compiler_params=_params(("parallel", "parallel", "arbitrary")),
        name="retention",
    )(proj3, proj3, proj3, proj3, cos_t, sin_t, log_gamma)


def _merge_body(ohg_ref, oret_ref, ga_ref, gb_ref, x_ref, wbh_ref, wbr_ref, wo_ref, nfw_ref, wq_ref,
                x1_ref, h2_ref, q_ref):
    yh = _dot(ohg_ref[...], wbh_ref[...])
    yr = _dot(oret_ref[...], wbr_ref[...])
    m = _sigmoid(ga_ref[...].astype(F32)) * yh + _sigmoid(gb_ref[...].astype(F32)) * yr
    x1 = x_ref[...] + _dot(m.astype(BF16), wo_ref[...])
    x1_ref[...] = x1
    ms = jnp.mean(x1 * x1, axis=-1, keepdims=True)
    h2 = (x1 * lax.rsqrt(ms + NORM_EPS) * nfw_ref[...]).astype(BF16)
    h2_ref[...] = h2
    q_ref[...] = _dot(h2, wq_ref[...])


def _merge(o_hg, o_ret, proj, x2, wbh, wbr, wo, nfw, wq, tm=256):
    t, d = x2.shape
    ga_blk = (IN_WIDTH - 2 * D_MODEL) // D_MODEL
    nq = wq.shape[1]

    def full(a):
        return pl.BlockSpec(a.shape, lambda i: (0, 0))

    return pl.pallas_call(
        _merge_body,
        grid=(t // tm,),
        in_specs=[
            pl.BlockSpec((tm, o_hg.shape[1]), lambda i: (i, 0)),
            pl.BlockSpec((tm, o_ret.shape[1]), lambda i: (i, 0)),
            pl.BlockSpec((tm, d), lambda i: (i, ga_blk)),
            pl.BlockSpec((tm, d), lambda i: (i, ga_blk + 1)),
            pl.BlockSpec((tm, d), lambda i: (i, 0)),
            full(wbh), full(wbr), full(wo), full(nfw), full(wq),
        ],
        out_specs=[
            pl.BlockSpec((tm, d), lambda i: (i, 0)),
            pl.BlockSpec((tm, d), lambda i: (i, 0)),
            pl.BlockSpec((tm, nq), lambda i: (i, 0)),
        ],
        out_shape=[
            jax.ShapeDtypeStruct((t, d), F32),
            jax.ShapeDtypeStruct((t, d), BF16),
            jax.ShapeDtypeStruct((t, nq), F32),
        ],
        compiler_params=_params(("parallel",)),
        name="merge",
    )(o_hg, o_ret, proj, proj, x2, wbh, wbr, wo, nfw, wq)


_CAND_NB = (16, 8, 5, 4, 3, 2, 2, 2)
_CAND_ROWS = 16 + 8 * 7 + 8


def _extract_top(vals, n_rows, rounds, on_round):
    iota = lax.broadcasted_iota(jnp.int32, vals.shape, 0).astype(F32)
    for r in range(rounds):
        m = jnp.max(vals, axis=0, keepdims=True)
        idx = jnp.min(jnp.where(vals == m, iota, float(n_rows)), axis=0, keepdims=True)
        sel = iota == idx
        vals = jnp.where(sel, -jnp.inf, vals)
        on_round(r, m, sel)
    return vals


def _topk_body(q_ref, keys_ref, e1_ref, n1_ref, e2_ref, r2_ref, s_scr, r_scr, top_scr, *, n_groups):
    nk, k = PEER_NKEYS, PEER_TOPK
    lanes = 128
    qb = q_ref[...].astype(BF16)
    for p in range(2):
        s_scr[p] = _dot_nt(keys_ref[p], qb[:, p * PEER_HALF:(p + 1) * PEER_HALF])

    r8 = lax.broadcasted_iota(jnp.int32, (8, lanes), 0)

    def group(gi, carry):
        ln = pl.ds(pl.multiple_of(gi * lanes, lanes), lanes)

        for p in range(2):
            rank = [jnp.full((nk, lanes), NOT_SELECTED_RANK, F32)]

            def on_round(r, m, sel, p=p, rank=rank):
                top_scr[p, r:r + 1, :] = m
                rank[0] = jnp.where(sel, float(r), rank[0])

            _extract_top(s_scr[p, :, ln], nk, k, on_round)
            r_scr[p] = rank[0]

        t1 = top_scr[0]
        t2 = top_scr[1]

        pieces = [t2 + t1[0:1, :], t2[0:8] + t1[1:2, :]]
        for a in range(2, 8):
            pieces.append(jnp.where(r8 < _CAND_NB[a], t2[0:8] + t1[a:a + 1, :], -jnp.inf))
        pieces.append(t1[8:16] + t2[0:1, :])
        cand = jnp.concatenate(pieces, axis=0)
        st2 = {"sel": jnp.zeros(cand.shape, F32), "z": None, "best": None}

        def on_round2(r, m, sel):
            st2["sel"] = jnp.where(sel, 1.0, st2["sel"])
            if r == 0:
                st2["best"] = m
                st2["z"] = jnp.ones_like(m)
            else:
                st2["z"] = st2["z"] + jnp.exp(m - st2["best"])

        _extract_top(cand, _CAND_ROWS, k, on_round2)
        selm = st2["sel"]
        inv_z = 1.0 / st2["z"]
        n_rows = [jnp.sum(selm[0:16], axis=0, keepdims=True)]
        for a in range(1, 8):
            lo = 16 + 8 * (a - 1)
            n_rows.append(jnp.sum(selm[lo:lo + 8], axis=0, keepdims=True))
        for i in range(8):
            n_rows.append(selm[_CAND_ROWS - 8 + i:_CAND_ROWS - 7 + i, :])

        rank1 = r_scr[0]
        n1 = jnp.zeros((nk, lanes), F32)
        for a in range(k):
            n1 = jnp.where(rank1 == float(a), n_rows[a], n1)
        n1_ref[:, ln] = n1
        e1_ref[:, ln] = jnp.exp(s_scr[0, :, ln] - t1[0:1, :]) * inv_z
        e2_ref[:, ln] = jnp.exp(s_scr[1, :, ln] - t2[0:1, :])
        r2_ref[:, ln] = r_scr[1]
        return carry

    lax.fori_loop(0, n_groups, group, 0)


def _peer_topk(q, keys_bf, tb=512):
    t = q.shape[0]
    nh = PEER_HEADS
    spec_o = pl.BlockSpec((None, PEER_NKEYS, tb), lambda i, h: (h, 0, i))
    shp = jax.ShapeDtypeStruct((nh, PEER_NKEYS, t), F32)
    return pl.pallas_call(
        functools.partial(_topk_body, n_groups=tb // 128),
        grid=(t // tb, nh),
        in_specs=[
            pl.BlockSpec((tb, 2 * PEER_HALF), lambda i, h: (i, h)),
            pl.BlockSpec((None, 2, PEER_NKEYS, PEER_HALF), lambda i, h: (h, 0, 0, 0)),
        ],
        out_specs=[spec_o, spec_o, spec_o, spec_o],
        out_shape=[shp, shp, shp, shp],
        scratch_shapes=[
            pltpu.VMEM((2, PEER_NKEYS, tb), F32),
            pltpu.VMEM((2, PEER_NKEYS, 128), F32),
            pltpu.VMEM((2, PEER_TOPK, 128), F32),
        ],
        compiler_params=_params(("parallel", "parallel")),
        name="peer_topk",
    )(q, keys_bf)


def _gelu_tanh(x):
    return 0.5 * x * (1.0 + jnp.tanh(0.7978845608028654 * (x + 0.044715 * (x * x * x))))


def _experts_body(h2_ref, u_ref, vt_ref, e1_ref, n1_ref, e2_ref, r2_ref, x1_ref, fnw_ref, o_ref,
                  acc_ref, p_ref, *, i1_per_blk):
    j = pl.program_id(1)
    nk = PEER_NKEYS

    @pl.when(j == 0)
    def _():
        acc_ref[...] = jnp.zeros_like(acc_ref)

    act = _gelu_tanh(_dot_nt(u_ref[...], h2_ref[...]))
    for il in range(i1_per_blk):
        w = None
        for h in range(PEER_HEADS):
            n1 = n1_ref[h, il:il + 1, :]
            e1 = e1_ref[h, il:il + 1, :]
            wh = jnp.where(r2_ref[h] < n1, e2_ref[h] * e1, 0.0)
            w = wh if w is None else w + wh
        p_ref[il * nk:(il + 1) * nk, :] = (w * act[il * nk:(il + 1) * nk, :]).astype(BF16)
    acc_ref[...] += _dot(vt_ref[...], p_ref[...])

    @pl.when(j == pl.num_programs(1) - 1)
    def _():
        y = x1_ref[...] + acc_ref[...].T
        ms = jnp.mean(y * y, axis=-1, keepdims=True)
        o_ref[...] = y * lax.rsqrt(ms + NORM_EPS) * fnw_ref[...]


def _peer_experts(h2, u_bf, vt_bf, e1, n1, e2, r2, x1, fnw, tb=512, eb=1024):
    t, d = x1.shape
    ne = u_bf.shape[0]
    nh, nk = PEER_HEADS, PEER_NKEYS
    i1b = eb // nk
    return pl.pallas_call(
        functools.partial(_experts_body, i1_per_blk=i1b),
        grid=(t // tb, ne // eb),
        in_specs=[
            pl.BlockSpec((tb, d), lambda i, j: (i, 0)),
            pl.BlockSpec((eb, d), lambda i, j: (j, 0)),
            pl.BlockSpec((d, eb), lambda i, j: (0, j)),
            pl.BlockSpec((nh, i1b, tb), lambda i, j: (0, j, i)),
            pl.BlockSpec((nh, i1b, tb), lambda i, j: (0, j, i)),
            pl.BlockSpec((nh, nk, tb), lambda i, j: (0, 0, i)),
            pl.BlockSpec((nh, nk, tb), lambda i, j: (0, 0, i)),
            pl.BlockSpec((tb, d), lambda i, j: (i, 0)),
            pl.BlockSpec((1, d), lambda i, j: (0, 0)),
        ],
        out_specs=pl.BlockSpec((tb, d), lambda i, j: (i, 0)),
        out_shape=jax.ShapeDtypeStruct((t, d), F32),
        scratch_shapes=[pltpu.VMEM((d, tb), F32), pltpu.VMEM((eb, tb), BF16)],
        compiler_params=_params(("parallel", "arbitrary")),
        name="peer_experts",
    )(h2, u_bf, vt_bf, e1, n1, e2, r2, x1, fnw)


def _rope_tables(s):
    half = RET_DK // 2
    inv = ROPE_BASE ** (-jnp.arange(half, dtype=F32) / half)
    ang = jnp.arange(s, dtype=F32)[:, None] * inv[None, :]
    return jnp.cos(ang), jnp.sin(ang)


def kernel(x, norm_mix_w, w_in, hg_lower_bounds, hg_norm_w, w_branch_hg, w_branch_ret, w_out, norm_ffn_w,
           peer_w_q, peer_sub_keys, expert_u, expert_v, final_norm_w):
    b, s, d = x.shape
    t = b * s
    x2 = x.reshape(t, d)

    proj = _inproj(x2, norm_mix_w[0:1], w_in[0].astype(BF16), F32)
    proj3 = proj.reshape(b, s, IN_WIDTH)

    o_hg = _hgrn2(proj3, hg_lower_bounds, hg_norm_w[0:1])

    cos_t, sin_t = _rope_tables(s)
    log_gamma = jnp.log(1.0 - jnp.exp2(-5.0 - jnp.arange(RET_HEADS, dtype=F32)))
    log_gamma = jnp.broadcast_to(log_gamma[:, None, None], (RET_HEADS, 1, RET_DV))
    o_ret = _retention(proj3, cos_t, sin_t, log_gamma)

    x1, h2, q = _merge(
        o_hg.reshape(t, -1), o_ret.reshape(t, -1), proj, x2,
        w_branch_hg[0].astype(BF16), w_branch_ret[0].astype(BF16), w_out[0].astype(BF16),
        norm_ffn_w[0:1], peer_w_q[0].astype(BF16))

    e1, n1, e2, r2 = _peer_topk(q, peer_sub_keys[0].astype(BF16))

    out = _peer_experts(h2, expert_u[0].astype(BF16), expert_v[0].T.astype(BF16), e1, n1, e2, r2, x1,
                        final_norm_w[None, :])
    return out.reshape(b, s, d)
```

```python
import functools

import jax
import jax.numpy as jnp
from jax import lax
from jax.experimental import pallas as pl
from jax.experimental.pallas import tpu as pltpu

F32 = jnp.float32
BF16 = jnp.bfloat16

D_MODEL = 1024
HG_HEADS = 8
HG_DK = 128
HG_DV = 128
RET_HEADS = 4
RET_DK = 256
RET_DV = 512
RET_CHUNK = 128
ROPE_BASE = 10000.0
PEER_HEADS = 8
PEER_NKEYS = 128
PEER_HALF = 128
PEER_TOPK = 16
NORM_EPS = 1e-6
IN_WIDTH = 12288

VMEM_LIMIT_BYTES = 56 * 1024 * 1024

HG_CHUNK = 64
HG_SUB = 16
NOT_SELECTED_RANK = 99.0
BF16_ROWS = 16
LANES = 128


def _sigmoid(x):
    return 1.0 / (1.0 + jnp.exp(-x))


def _dot(a, b):
    return jnp.dot(a, b, preferred_element_type=F32)


def _dot_nt(a, b):
    return lax.dot_general(a, b, (((1,), (1,)), ((), ())), preferred_element_type=F32)


def _dot_tn(a, b):
    return lax.dot_general(a, b, (((0,), (0,)), ((), ())), preferred_element_type=F32)


def _params(sem):
    return pltpu.CompilerParams(dimension_semantics=sem, vmem_limit_bytes=VMEM_LIMIT_BYTES)


def _inproj_body(x_ref, nw_ref, w_ref, o_ref, h_scr):
    @pl.when(pl.program_id(1) == 0)
    def _():
        x = x_ref[...]
        ms = jnp.mean(x * x, axis=-1, keepdims=True)
        h_scr[...] = (x * lax.rsqrt(ms + NORM_EPS) * nw_ref[...]).astype(BF16)

    o_ref[...] = _dot(h_scr[...], w_ref[...]).astype(o_ref.dtype)


def _inproj(x2, norm_w, w_bf, out_dtype, tm=1024, tn=1536):
    t, d = x2.shape
    n = w_bf.shape[1]
    return pl.pallas_call(
        _inproj_body,
        grid=(t // tm, n // tn),
        in_specs=[
            pl.BlockSpec((tm, d), lambda i, j: (i, 0)),
            pl.BlockSpec((1, d), lambda i, j: (0, 0)),
            pl.BlockSpec((d, tn), lambda i, j: (0, j)),
        ],
        out_specs=pl.BlockSpec((tm, tn), lambda i, j: (i, j)),
        out_shape=jax.ShapeDtypeStruct((t, n), out_dtype),
        scratch_shapes=[pltpu.VMEM((tm, d), BF16)],
        compiler_params=_params(("parallel", "arbitrary")),
        name="inproj",
    )(x2, norm_w, w_bf)


def _rows4(vals, n):
    return jnp.concatenate([jnp.broadcast_to(v, (HG_SUB, n)) for v in vals], axis=0)


def _hgrn2_body(q_ref, f_ref, i_ref, g_ref, lbp_ref, nw_ref, o_ref, st_ref, *, n_chunks, heads):
    c, sub, dk = HG_CHUNK, HG_SUB, HG_DK

    @pl.when(pl.program_id(2) == 0)
    def _():
        st_ref[...] = jnp.zeros_like(st_ref)

    lbp = lbp_ref[...]
    mx = jnp.max(lbp, axis=0, keepdims=True)
    el = jnp.exp(lbp - mx)
    lb_all = el[0:1, :] / (el[0:1, :] + el[1:2, :])
    nw_all = nw_ref[...]

    rin = lax.broadcasted_iota(jnp.int32, (c, dk), 0) & (sub - 1)
    r16 = lax.broadcasted_iota(jnp.int32, (sub, dk), 0)
    sub_shift = sub.bit_length() - 1
    rb = lax.broadcasted_iota(jnp.int32, (c, c), 0) >> sub_shift
    cb = lax.broadcasted_iota(jnp.int32, (c, c), 1) >> sub_shift
    m16 = ((rb & 1) == 1) & (cb == rb - 1)
    m32 = (rb >= 2) & (cb < 2)
    ones_b = jnp.ones((dk, dk), BF16)
    zero = jnp.zeros((1, dk), F32)
    one = jnp.ones((1, dk), F32)

    def head_chunk(hh, sl):
        hc = slice(hh * dk, (hh + 1) * dk)
        lb, nw = lb_all[:, hc], nw_all[:, hc]
        q = q_ref[sl, hc].astype(F32) * (HG_DK ** -0.5)
        fl = f_ref[sl, hc].astype(F32)
        v = i_ref[sl, hc].astype(F32)
        f = lb + (1.0 - lb) * _sigmoid(fl)
        k = 1.0 - f
        lf = jnp.log(f)

        bl = lf
        for sh in (1, 2, 4, 8):
            bl = bl + jnp.where(rin >= sh, pltpu.roll(bl, sh, 0), 0.0)
        g = [bl[sub * j + sub - 1:sub * j + sub, :] for j in range(4)]
        eg = [jnp.exp(x) for x in g]

        q16 = q * jnp.exp(bl)
        k16 = k * jnp.exp(_rows4(g, dk) - bl)
        q32 = q16 * _rows4([one, one, one, eg[2]], dk)
        k32 = k16 * _rows4([eg[1], one, one, one], dk)
        qin = q16 * _rows4([one, eg[0], eg[0] * eg[1], eg[0] * eg[1] * eg[2]], dk)
        kst = k16 * _rows4([eg[1] * eg[2] * eg[3], eg[2] * eg[3], eg[3], one], dk)

        s16 = _dot_nt(q16.astype(BF16), k16.astype(BF16))
        s32 = _dot_nt(q32.astype(BF16), k32.astype(BF16))
        a = jnp.where(m16, s16, 0.0) + jnp.where(m32, s32, 0.0)
        vb = v.astype(BF16)
        st = st_ref[hh]
        o = _dot(a.astype(BF16), vb) + _dot_nt(qin.astype(BF16), st.astype(BF16))

        od = []
        for j in range(4):
            lo = sub * j
            blj, qj, kj, vj = bl[lo:lo + sub], q[lo:lo + sub], k[lo:lo + sub], v[lo:lo + sub]
            ps = []
            for s in range(sub):
                e = jnp.exp(jnp.where(r16 >= s, blj - blj[s:s + 1, :], -jnp.inf))
                ps.append(qj * kj[s:s + 1, :] * e)
            p = jnp.concatenate(ps, axis=0).astype(BF16)
            ar = _dot(p, ones_b)
            acc = ar[0:sub] * vj[0:1, :]
            for s in range(1, sub):
                acc = acc + ar[s * sub:(s + 1) * sub] * vj[s:s + 1, :]
            od.append(acc)
        o = o + jnp.concatenate(od, axis=0)

        st_ref[hh] = st * (eg[0] * eg[1] * eg[2] * eg[3]) + _dot_tn(vb, kst.astype(BF16))

        ms = jnp.mean(o * o, axis=-1, keepdims=True)
        gt = g_ref[sl, hc].astype(F32)
        o_ref[sl, hc] = (o * lax.rsqrt(ms + NORM_EPS) * nw * (gt * _sigmoid(gt))).astype(o_ref.dtype)

    def chunk(ci, carry):
        sl = pl.ds(pl.multiple_of(ci * c, c), c)
        for hh in range(heads):
            head_chunk(hh, sl)
        return carry

    lax.fori_loop(0, n_chunks, chunk, 0)


def _hgrn2(proj3, lb_logits, norm_w, ts=512, heads_per_step=4):
    b, s, _ = proj3.shape
    nh, hps = HG_HEADS, heads_per_step
    w = hps * HG_DK
    ng = nh // hps

    def col(off):
        return pl.BlockSpec((None, ts, w), lambda bi, h, si, off=off: (bi, si, off + h))

    return pl.pallas_call(
        functools.partial(_hgrn2_body, n_chunks=ts // HG_CHUNK, heads=hps),
        grid=(b, ng, s // ts),
        in_specs=[
            col(0), col(ng), col(2 * ng), col(3 * ng),
            pl.BlockSpec((2, w), lambda bi, h, si: (0, h)),
            pl.BlockSpec((1, w), lambda bi, h, si: (0, h)),
        ],
        out_specs=pl.BlockSpec((None, ts, w), lambda bi, h, si: (bi, si, h)),
        out_shape=jax.ShapeDtypeStruct((b, s, nh * HG_DV), BF16),
        scratch_shapes=[pltpu.VMEM((hps, HG_DV, HG_DK), F32)],
        compiler_params=_params(("parallel", "parallel", "arbitrary")),
        name="hgrn2",
    )(proj3, proj3, proj3, proj3, lb_logits, norm_w)


def _ret_body(q_ref, k_ref, v_ref, g_ref, cos_ref, sin_ref, lg_ref, o_ref, st_ref, *, n_chunks):
    c, half = RET_CHUNK, RET_DK // 2

    @pl.when(pl.program_id(2) == 0)
    def _():
        st_ref[...] = jnp.zeros_like(st_ref)

    lgw = lg_ref[...]
    lg = lgw[:, :half]
    ri = lax.broadcasted_iota(jnp.int32, (c, half), 0).astype(F32)
    ci = lax.broadcasted_iota(jnp.int32, (c, half), 1).astype(F32)
    rel = ri - ci
    dmask = jnp.where(rel >= 0, jnp.exp(lg * jnp.maximum(rel, 0.0)), 0.0)
    qdec = jnp.exp(lg * (ri + 1.0))
    kdec = jnp.exp(lg * (c - 1.0 - ri))
    cdec = jnp.exp(lgw * float(c))

    def rot(x_ref, sl, cs, sn):
        x1 = x_ref[sl, :half].astype(F32)
        x2 = x_ref[sl, half:].astype(F32)
        return x1 * cs - x2 * sn, x1 * sn + x2 * cs

    def cat(x1, x2):
        return jnp.concatenate([x1, x2], axis=1).astype(BF16)

    def chunk(ci_, carry):
        sl = pl.ds(pl.multiple_of(ci_ * c, c), c)
        cs = cos_ref[sl, :]
        sn = sin_ref[sl, :]
        q1, q2 = rot(q_ref, sl, cs, sn)
        k1, k2 = rot(k_ref, sl, cs, sn)
        k1 = k1 * (RET_DK ** -0.5)
        k2 = k2 * (RET_DK ** -0.5)
        vb = v_ref[sl, :].astype(BF16)
        st = st_ref[...]

        a = _dot_nt(cat(q1, q2), cat(k1, k2)) * dmask
        o = _dot(a.astype(BF16), vb) + _dot(cat(q1 * qdec, q2 * qdec), st.astype(BF16))
        st_ref[...] = cdec * st + _dot_tn(cat(k1 * kdec, k2 * kdec), vb)

        ms = jnp.mean(o * o, axis=-1, keepdims=True)
        gt = g_ref[sl, :].astype(F32)
        o_ref[sl, :] = (o * lax.rsqrt(ms + NORM_EPS) * (gt * _sigmoid(gt))).astype(o_ref.dtype)
        return carry

    lax.fori_loop(0, n_chunks, chunk, 0)


def _retention(proj3, cos_t, sin_t, log_gamma, ts=512):
    b, s, _ = proj3.shape
    nh = RET_HEADS
    q_off = 4 * HG_HEADS * HG_DK
    qb, kb = q_off // RET_DK, q_off // RET_DK + nh
    vb, gb = (q_off + 2 * nh * RET_DK) // RET_DV, (q_off + 2 * nh * RET_DK) // RET_DV + nh

    def col(w, off):
        return pl.BlockSpec((None, ts, w), lambda bi, h, si, off=off: (bi, si, off + h))

    return pl.pallas_call(
        functools.partial(_ret_body, n_chunks=ts // RET_CHUNK),
        grid=(b, nh, s // ts),
        in_specs=[
            col(RET_DK, qb), col(RET_DK, kb), col(RET_DV, vb), col(RET_DV, gb),
            pl.BlockSpec((ts, RET_DK // 2), lambda bi, h, si: (si, 0)),
            pl.BlockSpec((ts, RET_DK // 2), lambda bi, h, si: (si, 0)),
            pl.BlockSpec((None, 1, RET_DV), lambda bi, h, si: (h, 0, 0)),
        ],
        out_specs=pl.BlockSpec((None, ts, RET_DV), lambda bi, h, si: (bi, si, h)),
        out_shape=jax.ShapeDtypeStruct((b, s, nh * RET_DV), BF16),
        scratch_shapes=[pltpu.VMEM((RET_DK, RET_DV), F32)],
        compiler_params=_params(("parallel", "parallel", "arbitrary")),
        name="retention",
    )(proj3, proj3, proj3, proj3, cos_t, sin_t, log_gamma)


def _merge_body(ohg_ref, oret_ref, ga_ref, gb_ref, x_ref, wbh_ref, wbr_ref, wo_ref, nfw_ref, wq_ref,
                x1_ref, h2t_ref, q_ref):
    yh = _dot(ohg_ref[...], wbh_ref[...])
    yr = _dot(oret_ref[...], wbr_ref[...])
    m = _sigmoid(ga_ref[...].astype(F32)) * yh + _sigmoid(gb_ref[...].astype(F32)) * yr
    x1 = x_ref[...] + _dot(m.astype(BF16), wo_ref[...])
    x1_ref[...] = x1
    ms = jnp.mean(x1 * x1, axis=-1, keepdims=True)
    h2 = x1 * lax.rsqrt(ms + NORM_EPS) * nfw_ref[...]
    h2t_ref[...] = h2.T.astype(BF16)
    q_ref[...] = _dot(h2.astype(BF16), wq_ref[...])


def _merge(o_hg, o_ret, proj, x2, wbh, wbr, wo, nfw, wq, tm=512):
    t, d = x2.shape
    ga_blk = (IN_WIDTH - 2 * D_MODEL) // D_MODEL
    nq = wq.shape[1]

    def full(a):
        return pl.BlockSpec(a.shape, lambda i: (0, 0), pipeline_mode=pl.Buffered(1))

    return pl.pallas_call(
        _merge_body,
        grid=(t // tm,),
        in_specs=[
            pl.BlockSpec((tm, o_hg.shape[1]), lambda i: (i, 0)),
            pl.BlockSpec((tm, o_ret.shape[1]), lambda i: (i, 0)),
            pl.BlockSpec((tm, d), lambda i: (i, ga_blk)),
            pl.BlockSpec((tm, d), lambda i: (i, ga_blk + 1)),
            pl.BlockSpec((tm, d), lambda i: (i, 0)),
            full(wbh), full(wbr), full(wo), full(nfw), full(wq),
        ],
        out_specs=[
            pl.BlockSpec((tm, d), lambda i: (i, 0)),
            pl.BlockSpec((d, tm), lambda i: (0, i)),
            pl.BlockSpec((tm, nq), lambda i: (i, 0)),
        ],
        out_shape=[
            jax.ShapeDtypeStruct((t, d), F32),
            jax.ShapeDtypeStruct((d, t), BF16),
            jax.ShapeDtypeStruct((t, nq), F32),
        ],
        compiler_params=_params(("parallel",)),
        name="merge",
    )(o_hg, o_ret, proj, proj, x2, wbh, wbr, wo, nfw, wq)


_CAND_NB = (16, 8, 5, 4, 3, 2, 2, 2)
_CAND_ROWS = 16 + 8 * 7 + 8


def _extract_top(vals, n_rows, rounds, on_round, break_ties):
    iota = lax.broadcasted_iota(jnp.int32, vals.shape, 0).astype(F32)
    for r in range(rounds):
        m = jnp.max(vals, axis=0, keepdims=True)
        sel = vals == m
        if break_ties:
            idx = jnp.min(jnp.where(sel, iota, float(n_rows)), axis=0, keepdims=True)
            sel = iota == idx
        vals = jnp.where(sel, -jnp.inf, vals)
        on_round(r, m, sel)
    return vals


def _topk_body(q_ref, keys_ref, e1_ref, n1_ref, e2_ref, r2_ref, s_scr, r_scr, top_scr, *, n_groups):
    nk, k = PEER_NKEYS, PEER_TOPK
    lanes = LANES
    qb = q_ref[...].astype(BF16)
    for p in range(2):
        s_scr[p] = _dot_nt(keys_ref[p], qb[:, p * PEER_HALF:(p + 1) * PEER_HALF])

    r8 = lax.broadcasted_iota(jnp.int32, (8, lanes), 0)

    def select(ln, break_ties):
        excess = jnp.zeros((1, lanes), F32)

        for p in range(2):
            rank = [jnp.full((nk, lanes), NOT_SELECTED_RANK, F32)]

            def on_round(r, m, sel, p=p, rank=rank):
                top_scr[p, r:r + 1, :] = m
                rank[0] = jnp.where(sel, float(r), rank[0])

            left = _extract_top(s_scr[p, :, ln], nk, k, on_round, break_ties)
            r_scr[p] = rank[0]
            removed = jnp.sum(jnp.where(left == -jnp.inf, 1.0, 0.0), axis=0, keepdims=True)
            excess = excess + (removed - float(k))

        t1 = top_scr[0]
        t2 = top_scr[1]

        pieces = [t2 + t1[0:1, :], t2[0:8] + t1[1:2, :]]
        for a in range(2, 8):
            pieces.append(jnp.where(r8 < _CAND_NB[a], t2[0:8] + t1[a:a + 1, :], -jnp.inf))
        pieces.append(t1[8:16] + t2[0:1, :])
        cand = jnp.concatenate(pieces, axis=0)
        best = []
        left = _extract_top(cand, _CAND_ROWS, k, lambda r, m, sel: best.append(m), break_ties)
        selm = jnp.where((left == -jnp.inf) & (cand > -jnp.inf), 1.0, 0.0)
        z = jnp.ones_like(best[0])
        for m in best[1:]:
            z = z + jnp.exp(m - best[0])
        inv_z = 1.0 / z
        n_rows = [jnp.sum(selm[0:16], axis=0, keepdims=True)]
        for a in range(1, 8):
            lo = 16 + 8 * (a - 1)
            n_rows.append(jnp.sum(selm[lo:lo + 8], axis=0, keepdims=True))
        for i in range(8):
            n_rows.append(selm[_CAND_ROWS - 8 + i:_CAND_ROWS - 7 + i, :])
        total = n_rows[0]
        for x in n_rows[1:]:
            total = total + x
        excess = excess + (total - float(k))

        rank1 = r_scr[0]
        n1 = jnp.zeros((nk, lanes), F32)
        for a in range(k):
            n1 = jnp.where(rank1 == float(a), n_rows[a], n1)
        n1_ref[:, ln] = n1
        e1_ref[:, ln] = jnp.exp(s_scr[0, :, ln] - t1[0:1, :]) * inv_z
        e2_ref[:, ln] = jnp.exp(s_scr[1, :, ln] - t2[0:1, :]).astype(e2_ref.dtype)
        r2_ref[:, ln] = r_scr[1].astype(r2_ref.dtype)
        return excess

    def group(gi, carry):
        ln = pl.ds(pl.multiple_of(gi * lanes, lanes), lanes)
        excess = select(ln, break_ties=False)

        @pl.when(jnp.max(excess) > 0.0)
        def _():
            select(ln, break_ties=True)

        return carry

    lax.fori_loop(0, n_groups, group, 0)


def _peer_topk(q, keys_bf, tb=512):
    t = q.shape[0]
    nh = PEER_HEADS
    spec_o = pl.BlockSpec((None, PEER_NKEYS, tb), lambda i, h: (h, 0, i))
    shp = jax.ShapeDtypeStruct((nh, PEER_NKEYS, t), F32)
    shp_b = jax.ShapeDtypeStruct((nh, PEER_NKEYS, t), BF16)
    return pl.pallas_call(
        functools.partial(_topk_body, n_groups=tb // LANES),
        grid=(t // tb, nh),
        in_specs=[
            pl.BlockSpec((tb, 2 * PEER_HALF), lambda i, h: (i, h)),
            pl.BlockSpec((None, 2, PEER_NKEYS, PEER_HALF), lambda i, h: (h, 0, 0, 0)),
        ],
        out_specs=[spec_o, spec_o, spec_o, spec_o],
        out_shape=[shp, shp, shp_b, shp_b],
        scratch_shapes=[
            pltpu.VMEM((2, PEER_NKEYS, tb), F32),
            pltpu.VMEM((2, PEER_NKEYS, LANES), F32),
            pltpu.VMEM((2, PEER_TOPK, LANES), F32),
        ],
        compiler_params=_params(("parallel", "parallel")),
        name="peer_topk",
    )(q, keys_bf)


_GELU_K0 = -2.0 * 0.7978845608028654 * 1.4426950408889634
_GELU_K1 = _GELU_K0 * 0.044715


def _gelu_tanh(x):
    return x / (1.0 + jnp.exp2(x * (x * x * _GELU_K1 + _GELU_K0)))


def _experts_body(h2t_ref, u_ref, vt_ref, e1_ref, n1_ref, e2_ref, r2_ref, x1_ref, fnw_ref, o_ref,
                  acc_ref, p_ref, *, i1_per_blk):
    j = pl.program_id(1)
    nk = PEER_NKEYS

    @pl.when(j == 0)
    def _():
        acc_ref[...] = jnp.zeros_like(acc_ref)

    act = _gelu_tanh(_dot(u_ref[...], h2t_ref[...])).astype(BF16)
    tb = act.shape[1]
    zero = jnp.zeros((BF16_ROWS, tb), BF16)
    for il in range(i1_per_blk):
        n1 = [jnp.broadcast_to(n1_ref[h, il:il + 1, :], (BF16_ROWS, tb)).astype(BF16) for h in range(PEER_HEADS)]
        e1 = [jnp.broadcast_to(e1_ref[h, il:il + 1, :], (BF16_ROWS, tb)).astype(BF16) for h in range(PEER_HEADS)]
        for m in range(nk // BF16_ROWS):
            rows = slice(m * BF16_ROWS, (m + 1) * BF16_ROWS)
            w = None
            for h in range(PEER_HEADS):
                sel = r2_ref[h, rows, :] < n1[h]
                prod = e2_ref[h, rows, :] * e1[h]
                w = jnp.where(sel, prod, zero) if w is None else jnp.where(sel, w + prod, w)
            lo = il * nk + m * BF16_ROWS
            p_ref[lo:lo + BF16_ROWS, :] = w * act[lo:lo + BF16_ROWS, :]
    acc_ref[...] += _dot(vt_ref[...], p_ref[...])

    @pl.when(j == pl.num_programs(1) - 1)
    def _():
        y = x1_ref[...] + acc_ref[...].T
        ms = jnp.mean(y * y, axis=-1, keepdims=True)
        o_ref[...] = y * lax.rsqrt(ms + NORM_EPS) * fnw_ref[...]


def _peer_experts(h2t, u_bf, vt_bf, e1, n1, e2, r2, x1, fnw, tb=512, eb=1024):
    t, d = x1.shape
    ne = u_bf.shape[0]
    nh, nk = PEER_HEADS, PEER_NKEYS
    i1b = eb // nk
    return pl.pallas_call(
        functools.partial(_experts_body, i1_per_blk=i1b),
        grid=(t // tb, ne // eb),
        in_specs=[
            pl.BlockSpec((d, tb), lambda i, j: (0, i)),
            pl.BlockSpec((eb, d), lambda i, j: (j, 0)),
            pl.BlockSpec((d, eb), lambda i, j: (0, j)),
            pl.BlockSpec((nh, i1b, tb), lambda i, j: (0, j, i)),
            pl.BlockSpec((nh, i1b, tb), lambda i, j: (0, j, i)),
            pl.BlockSpec((nh, nk, tb), lambda i, j: (0, 0, i)),
            pl.BlockSpec((nh, nk, tb), lambda i, j: (0, 0, i)),
            pl.BlockSpec((tb, d), lambda i, j: (i, 0)),
            pl.BlockSpec((1, d), lambda i, j: (0, 0)),
        ],
        out_specs=pl.BlockSpec((tb, d), lambda i, j: (i, 0)),
        out_shape=jax.ShapeDtypeStruct((t, d), F32),
        scratch_shapes=[pltpu.VMEM((d, tb), F32), pltpu.VMEM((eb, tb), BF16)],
        compiler_params=_params(("parallel", "arbitrary")),
        name="peer_experts",
    )(h2t, u_bf, vt_bf, e1, n1, e2, r2, x1, fnw)


def _rope_tables(s):
    half = RET_DK // 2
    inv = ROPE_BASE ** (-jnp.arange(half, dtype=F32) / half)
    ang = jnp.arange(s, dtype=F32)[:, None] * inv[None, :]
    return jnp.cos(ang), jnp.sin(ang)


def kernel(x, norm_mix_w, w_in, hg_lower_bounds, hg_norm_w, w_branch_hg, w_branch_ret, w_out, norm_ffn_w,
           peer_w_q, peer_sub_keys, expert_u, expert_v, final_norm_w):
    b, s, d = x.shape
    t = b * s
    x2 = x.reshape(t, d)

    proj = _inproj(x2, norm_mix_w[0:1], w_in[0].astype(BF16), BF16)
    proj3 = proj.reshape(b, s, IN_WIDTH)

    o_hg = _hgrn2(proj3, hg_lower_bounds, hg_norm_w[0:1])

    cos_t, sin_t = _rope_tables(s)
    log_gamma = jnp.log(1.0 - jnp.exp2(-5.0 - jnp.arange(RET_HEADS, dtype=F32)))
    log_gamma = jnp.broadcast_to(log_gamma[:, None, None], (RET_HEADS, 1, RET_DV))
    o_ret = _retention(proj3, cos_t, sin_t, log_gamma)

    x1, h2t, q = _merge(
        o_hg.reshape(t, -1), o_ret.reshape(t, -1), proj, x2,
        w_branch_hg[0].astype(BF16), w_branch_ret[0].astype(BF16), w_out[0].astype(BF16),
        norm_ffn_w[0:1], peer_w_q[0].astype(BF16))

    e1, n1, e2, r2 = _peer_topk(q, peer_sub_keys[0].astype(BF16))

    out = _peer_experts(h2t, expert_u[0].astype(BF16), expert_v[0].T.astype(BF16), e1, n1, e2, r2, x1,
                        final_norm_w[None, :])
    return out.reshape(b, s, d)
```

```python
import functools

import jax
import jax.numpy as jnp
from jax import lax
from jax.experimental import pallas as pl
from jax.experimental.pallas import tpu as pltpu

F32 = jnp.float32
BF16 = jnp.bfloat16

D_MODEL = 1024
HG_HEADS = 8
HG_DK = 128
HG_DV = 128
RET_HEADS = 4
RET_DK = 256
RET_DV = 512
RET_CHUNK = 128
ROPE_BASE = 10000.0
PEER_HEADS = 8
PEER_NKEYS = 128
PEER_HALF = 128
PEER_TOPK = 16
NORM_EPS = 1e-6
IN_WIDTH = 12288

VMEM_LIMIT_BYTES = 56 * 1024 * 1024

HG_CHUNK = 64
HG_SUB = 16
NOT_SELECTED_RANK = 99.0
BF16_ROWS = 16
LANES = 128
SUBLANES = 8
LOG2E = 1.4426950408889634


def _sigmoid(x):
    return 1.0 / (1.0 + jnp.exp(-x))


def _dot(a, b):
    return jnp.dot(a, b, preferred_element_type=F32)


def _dot_nt(a, b):
    return lax.dot_general(a, b, (((1,), (1,)), ((), ())), preferred_element_type=F32)


def _dot_tn(a, b):
    return lax.dot_general(a, b, (((0,), (0,)), ((), ())), preferred_element_type=F32)


def _params(sem):
    return pltpu.CompilerParams(dimension_semantics=sem, vmem_limit_bytes=VMEM_LIMIT_BYTES)


def _inproj_body(x_ref, nw_ref, w_ref, o_ref, h_scr):
    @pl.when(pl.program_id(1) == 0)
    def _():
        x = x_ref[...]
        ms = jnp.mean(x * x, axis=-1, keepdims=True)
        h_scr[...] = (x * lax.rsqrt(ms + NORM_EPS) * nw_ref[...]).astype(BF16)

    o_ref[...] = _dot(h_scr[...], w_ref[...]).astype(o_ref.dtype)


def _inproj(x2, norm_w, w_bf, out_dtype, tm=1024, tn=1536):
    t, d = x2.shape
    n = w_bf.shape[1]
    return pl.pallas_call(
        _inproj_body,
        grid=(t // tm, n // tn),
        in_specs=[
            pl.BlockSpec((tm, d), lambda i, j: (i, 0)),
            pl.BlockSpec((1, d), lambda i, j: (0, 0)),
            pl.BlockSpec((d, tn), lambda i, j: (0, j)),
        ],
        out_specs=pl.BlockSpec((tm, tn), lambda i, j: (i, j)),
        out_shape=jax.ShapeDtypeStruct((t, n), out_dtype),
        scratch_shapes=[pltpu.VMEM((tm, d), BF16)],
        compiler_params=_params(("parallel", "arbitrary")),
        name="inproj",
    )(x2, norm_w, w_bf)


def _rows4(vals, n):
    return jnp.concatenate([jnp.broadcast_to(v, (HG_SUB, n)) for v in vals], axis=0)


def _hgrn2_body(q_ref, f_ref, i_ref, g_ref, lbp_ref, nw_ref, o_ref, st_ref, *, n_chunks, heads):
    c, sub, dk = HG_CHUNK, HG_SUB, HG_DK

    @pl.when(pl.program_id(2) == 0)
    def _():
        st_ref[...] = jnp.zeros_like(st_ref)

    lbp = lbp_ref[...]
    mx = jnp.max(lbp, axis=0, keepdims=True)
    el = jnp.exp(lbp - mx)
    lb_all = el[0:1, :] / (el[0:1, :] + el[1:2, :])
    nw_all = nw_ref[...]

    rin = lax.broadcasted_iota(jnp.int32, (c, dk), 0) & (sub - 1)
    r8 = lax.broadcasted_iota(jnp.int32, (SUBLANES, dk), 0)
    lane_c = lax.broadcasted_iota(jnp.int32, (SUBLANES, c), 1)
    sub_shift = sub.bit_length() - 1
    rb = lax.broadcasted_iota(jnp.int32, (c, c), 0) >> sub_shift
    cb = lax.broadcasted_iota(jnp.int32, (c, c), 1) >> sub_shift
    m16 = ((rb & 1) == 1) & (cb == rb - 1)
    m32 = (rb >= 2) & (cb < 2)
    one = jnp.ones((1, dk), F32)

    def head_chunk(hh, sl):
        hc = slice(hh * dk, (hh + 1) * dk)
        lb, nw = lb_all[:, hc], nw_all[:, hc]
        q = q_ref[sl, hc].astype(F32) * (HG_DK ** -0.5)
        fl = f_ref[sl, hc].astype(F32)
        v = i_ref[sl, hc].astype(F32)
        f = lb + (1.0 - lb) * _sigmoid(fl)
        k = 1.0 - f
        lf = jnp.log(f)

        bl = lf
        for sh in (1, 2, 4, 8):
            bl = bl + jnp.where(rin >= sh, pltpu.roll(bl, sh, 0), 0.0)
        g = [bl[sub * j + sub - 1:sub * j + sub, :] for j in range(4)]
        eg = [jnp.exp(x) for x in g]

        q16 = q * jnp.exp(bl)
        k16 = k * jnp.exp(_rows4(g, dk) - bl)
        q32 = q16 * _rows4([one, one, one, eg[2]], dk)
        k32 = k16 * _rows4([eg[1], one, one, one], dk)
        qin = q16 * _rows4([one, eg[0], eg[0] * eg[1], eg[0] * eg[1] * eg[2]], dk)
        kst = k16 * _rows4([eg[1] * eg[2] * eg[3], eg[2] * eg[3], eg[3], one], dk)

        s16 = _dot_nt(q16.astype(BF16), k16.astype(BF16))
        s32 = _dot_nt(q32.astype(BF16), k32.astype(BF16))
        a = jnp.where(m16, s16, 0.0) + jnp.where(m32, s32, 0.0)

        bl2 = bl * LOG2E
        tiles = []
        for j in range(c // sub):
            lo = sub * j
            for hf in range(sub // SUBLANES):
                r0 = lo + SUBLANES * hf
                b_t, q_t = bl2[r0:r0 + SUBLANES], q[r0:r0 + SUBLANES]
                tile = a[r0:r0 + SUBLANES, :]
                for s in range(SUBLANES * (hf + 1)):
                    d = b_t - bl2[lo + s:lo + s + 1, :]
                    if s > SUBLANES * hf:
                        d = jnp.where(r8 >= s - SUBLANES * hf, d, -jnp.inf)
                    p = q_t * k[lo + s:lo + s + 1, :] * jnp.exp2(d)
                    tile = jnp.where(lane_c == lo + s, jnp.sum(p, axis=-1, keepdims=True), tile)
                tiles.append(tile)
        a = jnp.concatenate(tiles, axis=0)

        vb = v.astype(BF16)
        st = st_ref[hh]
        o = _dot(a.astype(BF16), vb) + _dot_nt(qin.astype(BF16), st.astype(BF16))

        st_ref[hh] = st * (eg[0] * eg[1] * eg[2] * eg[3]) + _dot_tn(vb, kst.astype(BF16))

        ms = jnp.mean(o * o, axis=-1, keepdims=True)
        gt = g_ref[sl, hc].astype(F32)
        o_ref[sl, hc] = (o * lax.rsqrt(ms + NORM_EPS) * nw * (gt * _sigmoid(gt))).astype(o_ref.dtype)

    def chunk(ci, carry):
        sl = pl.ds(pl.multiple_of(ci * c, c), c)
        for hh in range(heads):
            head_chunk(hh, sl)
        return carry

    lax.fori_loop(0, n_chunks, chunk, 0)


def _hgrn2(proj3, lb_logits, norm_w, ts=512, heads_per_step=8):
    b, s, _ = proj3.shape
    nh, hps = HG_HEADS, heads_per_step
    w = hps * HG_DK
    ng = nh // hps

    def col(off):
        return pl.BlockSpec((None, ts, w), lambda bi, h, si, off=off: (bi, si, off + h))

    return pl.pallas_call(
        functools.partial(_hgrn2_body, n_chunks=ts // HG_CHUNK, heads=hps),
        grid=(b, ng, s // ts),
        in_specs=[
            col(0), col(ng), col(2 * ng), col(3 * ng),
            pl.BlockSpec((2, w), lambda bi, h, si: (0, h)),
            pl.BlockSpec((1, w), lambda bi, h, si: (0, h)),
        ],
        out_specs=pl.BlockSpec((None, ts, w), lambda bi, h, si: (bi, si, h)),
        out_shape=jax.ShapeDtypeStruct((b, s, nh * HG_DV), BF16),
        scratch_shapes=[pltpu.VMEM((hps, HG_DV, HG_DK), F32)],
        compiler_params=_params(("parallel", "parallel", "arbitrary")),
        name="hgrn2",
    )(proj3, proj3, proj3, proj3, lb_logits, norm_w)


def _ret_body(q_ref, k_ref, v_ref, g_ref, cos_ref, sin_ref, lg_ref, o_ref, st_ref, dm_ref, qd_ref, kd_ref,
              *, n_chunks):
    c, half, nh = RET_CHUNK, RET_DK // 2, RET_HEADS

    @pl.when(pl.program_id(1) == 0)
    def _():
        st_ref[...] = jnp.zeros_like(st_ref)
        ri = lax.broadcasted_iota(jnp.int32, (c, half), 0).astype(F32)
        ci = lax.broadcasted_iota(jnp.int32, (c, half), 1).astype(F32)
        rel = ri - ci
        for h in range(nh):
            lg = lg_ref[h][:, :half]
            dm_ref[h] = jnp.where(rel >= 0, jnp.exp(lg * jnp.maximum(rel, 0.0)), 0.0)
            qd_ref[h] = jnp.exp(lg * (ri + 1.0))
            kd_ref[h] = jnp.exp(lg * (c - 1.0 - ri))

    def cat(x1, x2):
        return jnp.concatenate([x1, x2], axis=1).astype(BF16)

    def head_chunk(h, sl, cs, sn):
        vc = slice(h * RET_DV, (h + 1) * RET_DV)

        def rot(x_ref):
            x1 = x_ref[sl, h * RET_DK:h * RET_DK + half].astype(F32)
            x2 = x_ref[sl, h * RET_DK + half:(h + 1) * RET_DK].astype(F32)
            return x1 * cs - x2 * sn, x1 * sn + x2 * cs

        q1, q2 = rot(q_ref)
        k1, k2 = rot(k_ref)
        k1 = k1 * (RET_DK ** -0.5)
        k2 = k2 * (RET_DK ** -0.5)
        vb = v_ref[sl, vc].astype(BF16)
        st = st_ref[h]
        qdec, kdec = qd_ref[h], kd_ref[h]
        cdec = jnp.exp(lg_ref[h] * float(c))

        a = _dot_nt(cat(q1, q2), cat(k1, k2)) * dm_ref[h]
        o = _dot(a.astype(BF16), vb) + _dot(cat(q1 * qdec, q2 * qdec), st.astype(BF16))
        st_ref[h] = cdec * st + _dot_tn(cat(k1 * kdec, k2 * kdec), vb)

        ms = jnp.mean(o * o, axis=-1, keepdims=True)
        gt = g_ref[sl, vc].astype(F32)
        o_ref[sl, vc] = (o * lax.rsqrt(ms + NORM_EPS) * (gt * _sigmoid(gt))).astype(o_ref.dtype)

    def chunk(ci_, carry):
        sl = pl.ds(pl.multiple_of(ci_ * c, c), c)
        cs = cos_ref[sl, :]
        sn = sin_ref[sl, :]
        for h in range(nh):
            head_chunk(h, sl, cs, sn)
        return carry

    lax.fori_loop(0, n_chunks, chunk, 0)


def _retention(proj3, cos_t, sin_t, log_gamma, ts=512):
    b, s, _ = proj3.shape
    nh = RET_HEADS
    wq, wv = nh * RET_DK, nh * RET_DV
    q_off = 4 * HG_HEADS * HG_DK
    qb, kb = q_off // wq, q_off // wq + 1
    vb, gb = (q_off + 2 * wq) // wv, (q_off + 2 * wq) // wv + 1
    half = RET_DK // 2

    def col(w, blk):
        return pl.BlockSpec((None, ts, w), lambda bi, si, blk=blk: (bi, si, blk))

    return pl.pallas_call(
        functools.partial(_ret_body, n_chunks=ts // RET_CHUNK),
        grid=(b, s // ts),
        in_specs=[
            col(wq, qb), col(wq, kb), col(wv, vb), col(wv, gb),
            pl.BlockSpec((ts, half), lambda bi, si: (si, 0)),
            pl.BlockSpec((ts, half), lambda bi, si: (si, 0)),
            pl.BlockSpec((nh, 1, RET_DV), lambda bi, si: (0, 0, 0)),
        ],
        out_specs=pl.BlockSpec((None, ts, wv), lambda bi, si: (bi, si, 0)),
        out_shape=jax.ShapeDtypeStruct((b, s, wv), BF16),
        scratch_shapes=[
            pltpu.VMEM((nh, RET_DK, RET_DV), F32),
            pltpu.VMEM((nh, RET_CHUNK, RET_CHUNK), F32),
            pltpu.VMEM((nh, RET_CHUNK, half), F32),
            pltpu.VMEM((nh, RET_CHUNK, half), F32),
        ],
        compiler_params=_params(("parallel", "arbitrary")),
        name="retention",
    )(proj3, proj3, proj3, proj3, cos_t, sin_t, log_gamma)


def _merge_body(ohg_ref, oret_ref, ga_ref, gb_ref, x_ref, wbh_ref, wbr_ref, wo_ref, nfw_ref, wq_ref,
                x1_ref, h2t_ref, q_ref):
    yh = _dot(ohg_ref[...], wbh_ref[...])
    yr = _dot(oret_ref[...], wbr_ref[...])
    m = _sigmoid(ga_ref[...].astype(F32)) * yh + _sigmoid(gb_ref[...].astype(F32)) * yr
    x1 = x_ref[...] + _dot(m.astype(BF16), wo_ref[...])
    x1_ref[...] = x1
    ms = jnp.mean(x1 * x1, axis=-1, keepdims=True)
    h2 = x1 * lax.rsqrt(ms + NORM_EPS) * nfw_ref[...]
    h2t_ref[...] = h2.T.astype(BF16)
    q_ref[...] = _dot(h2.astype(BF16), wq_ref[...])


def _merge(o_hg, o_ret, proj, x2, wbh, wbr, wo, nfw, wq, tm=512):
    t, d = x2.shape
    ga_blk = (IN_WIDTH - 2 * D_MODEL) // D_MODEL
    nq = wq.shape[1]

    def full(a):
        return pl.BlockSpec(a.shape, lambda i: (0, 0), pipeline_mode=pl.Buffered(1))

    return pl.pallas_call(
        _merge_body,
        grid=(t // tm,),
        in_specs=[
            pl.BlockSpec((tm, o_hg.shape[1]), lambda i: (i, 0)),
            pl.BlockSpec((tm, o_ret.shape[1]), lambda i: (i, 0)),
            pl.BlockSpec((tm, d), lambda i: (i, ga_blk)),
            pl.BlockSpec((tm, d), lambda i: (i, ga_blk + 1)),
            pl.BlockSpec((tm, d), lambda i: (i, 0)),
            full(wbh), full(wbr), full(wo), full(nfw), full(wq),
        ],
        out_specs=[
            pl.BlockSpec((tm, d), lambda i: (i, 0)),
            pl.BlockSpec((d, tm), lambda i: (0, i)),
            pl.BlockSpec((tm, nq), lambda i: (i, 0)),
        ],
        out_shape=[
            jax.ShapeDtypeStruct((t, d), F32),
            jax.ShapeDtypeStruct((d, t), BF16),
            jax.ShapeDtypeStruct((t, nq), F32),
        ],
        compiler_params=_params(("parallel",)),
        name="merge",
    )(o_hg, o_ret, proj, proj, x2, wbh, wbr, wo, nfw, wq)


_CAND_NB = (16, 8, 5, 4, 3, 2, 2, 2)
_CAND_ROWS = 16 + 8 * 7 + 8


def _extract_top(vals, n_rows, rounds, on_round, break_ties):
    iota = lax.broadcasted_iota(jnp.int32, vals.shape, 0).astype(F32)
    for r in range(rounds):
        m = jnp.max(vals, axis=0, keepdims=True)
        sel = vals == m
        if break_ties:
            idx = jnp.min(jnp.where(sel, iota, float(n_rows)), axis=0, keepdims=True)
            sel = iota == idx
        vals = jnp.where(sel, -jnp.inf, vals)
        on_round(r, m, sel)
    return vals


def _topk_body(q_ref, keys_ref, e1_ref, n1_ref, e2_ref, r2_ref, s_scr, w_scr, r_scr, top_scr, *, n_groups):
    nk, k = PEER_NKEYS, PEER_TOPK
    lanes = LANES
    qb = q_ref[...].astype(BF16)
    for p in range(2):
        s_scr[p] = _dot_nt(keys_ref[p], qb[:, p * PEER_HALF:(p + 1) * PEER_HALF])

    r8 = lax.broadcasted_iota(jnp.int32, (8, lanes), 0)
    iota_k = lax.broadcasted_iota(jnp.int32, (nk, lanes), 0).astype(F32)
    groups = [pl.ds(g * lanes, lanes) for g in range(n_groups)]

    def stage1(break_ties):
        w_scr[...] = s_scr[...]
        r_scr[...] = jnp.full(r_scr.shape, NOT_SELECTED_RANK, F32)
        removed_val = jnp.full((nk, lanes), -jnp.inf, F32)
        for r in range(k):
            rank_val = jnp.full((nk, lanes), float(r), F32)
            for p in range(2):
                for ln in groups:
                    vals = w_scr[p, :, ln]
                    m = jnp.max(vals, axis=0, keepdims=True)
                    sel = vals == m
                    if break_ties:
                        sel = iota_k == jnp.min(jnp.where(sel, iota_k, float(nk)), axis=0, keepdims=True)
                    pltpu.store(w_scr.at[p, :, ln], removed_val, mask=sel)
                    pltpu.store(r_scr.at[p, :, ln], rank_val, mask=sel)
                    top_scr[p, r:r + 1, ln] = m

    def select(ln, break_ties):
        excess = jnp.zeros((1, lanes), F32)
        for p in range(2):
            removed = jnp.sum(jnp.where(w_scr[p, :, ln] == -jnp.inf, 1.0, 0.0), axis=0, keepdims=True)
            excess = excess + (removed - float(k))

        t1 = top_scr[0, :, ln]
        t2 = top_scr[1, :, ln]

        pieces = [t2 + t1[0:1, :], t2[0:8] + t1[1:2, :]]
        for a in range(2, 8):
            pieces.append(jnp.where(r8 < _CAND_NB[a], t2[0:8] + t1[a:a + 1, :], -jnp.inf))
        pieces.append(t1[8:16] + t2[0:1, :])
        cand = jnp.concatenate(pieces, axis=0)
        best = []
        left = _extract_top(cand, _CAND_ROWS, k, lambda r, m, sel: best.append(m), break_ties)
        selm = jnp.where((left == -jnp.inf) & (cand > -jnp.inf), 1.0, 0.0)
        z = jnp.ones_like(best[0])
        for m in best[1:]:
            z = z + jnp.exp(m - best[0])
        inv_z = 1.0 / z
        n_rows = [jnp.sum(selm[0:16], axis=0, keepdims=True)]
        for a in range(1, 8):
            lo = 16 + 8 * (a - 1)
            n_rows.append(jnp.sum(selm[lo:lo + 8], axis=0, keepdims=True))
        for i in range(8):
            n_rows.append(selm[_CAND_ROWS - 8 + i:_CAND_ROWS - 7 + i, :])
        total = n_rows[0]
        for x in n_rows[1:]:
            total = total + x
        excess = excess + (total - float(k))

        rank1 = r_scr[0, :, ln]
        n1_ref[:, ln] = jnp.zeros((nk, lanes), F32)
        for a in range(k):
            pltpu.store(n1_ref.at[:, ln], jnp.broadcast_to(n_rows[a], (nk, lanes)), mask=rank1 == float(a))
        e1_ref[:, ln] = jnp.exp(s_scr[0, :, ln] - t1[0:1, :]) * inv_z
        e2_ref[:, ln] = jnp.exp(s_scr[1, :, ln] - t2[0:1, :]).astype(e2_ref.dtype)
        r2_ref[:, ln] = r_scr[1, :, ln].astype(r2_ref.dtype)
        return excess

    def run(break_ties):
        stage1(break_ties)
        excess = select(groups[0], break_ties)
        for ln in groups[1:]:
            excess = jnp.maximum(excess, select(ln, break_ties))
        return excess

    excess = run(break_ties=False)

    @pl.when(jnp.max(excess) > 0.0)
    def _():
        run(break_ties=True)


def _peer_topk(q, keys_bf, tb=512):
    t = q.shape[0]
    nh = PEER_HEADS
    spec_o = pl.BlockSpec((None, PEER_NKEYS, tb), lambda i, h: (h, 0, i))
    shp = jax.ShapeDtypeStruct((nh, PEER_NKEYS, t), F32)
    shp_b = jax.ShapeDtypeStruct((nh, PEER_NKEYS, t), BF16)
    return pl.pallas_call(
        functools.partial(_topk_body, n_groups=tb // LANES),
        grid=(t // tb, nh),
        in_specs=[
            pl.BlockSpec((tb, 2 * PEER_HALF), lambda i, h: (i, h)),
            pl.BlockSpec((None, 2, PEER_NKEYS, PEER_HALF), lambda i, h: (h, 0, 0, 0)),
        ],
        out_specs=[spec_o, spec_o, spec_o, spec_o],
        out_shape=[shp, shp, shp_b, shp_b],
        scratch_shapes=[
            pltpu.VMEM((2, PEER_NKEYS, tb), F32),
            pltpu.VMEM((2, PEER_NKEYS, tb), F32),
            pltpu.VMEM((2, PEER_NKEYS, tb), F32),
            pltpu.VMEM((2, PEER_TOPK, tb), F32),
        ],
        compiler_params=_params(("parallel", "parallel")),
        name="peer_topk",
    )(q, keys_bf)


_GELU_K0 = -2.0 * 0.7978845608028654 * 1.4426950408889634
_GELU_K1 = _GELU_K0 * 0.044715


def _gelu_tanh(x):
    return x / (1.0 + jnp.exp2(x * (x * x * _GELU_K1 + _GELU_K0)))


def _experts_body(h2t_ref, u_ref, vt_ref, e1_ref, n1_ref, e2_ref, r2_ref, x1_ref, fnw_ref, o_ref,
                  acc_ref, p_ref, *, i1_per_blk):
    j = pl.program_id(1)
    nk = PEER_NKEYS

    @pl.when(j == 0)
    def _():
        acc_ref[...] = jnp.zeros_like(acc_ref)

    act = _gelu_tanh(_dot(u_ref[...], h2t_ref[...])).astype(BF16)
    tb = act.shape[1]
    zero = jnp.zeros((BF16_ROWS, tb), BF16)
    for il in range(i1_per_blk):
        n1 = [jnp.broadcast_to(n1_ref[h, il:il + 1, :], (BF16_ROWS, tb)).astype(BF16) for h in range(PEER_HEADS)]
        e1 = [jnp.broadcast_to(e1_ref[h, il:il + 1, :], (BF16_ROWS, tb)).astype(BF16) for h in range(PEER_HEADS)]
        for m in range(nk // BF16_ROWS):
            rows = slice(m * BF16_ROWS, (m + 1) * BF16_ROWS)
            w = None
            for h in range(PEER_HEADS):
                sel = r2_ref[h, rows, :] < n1[h]
                prod = e2_ref[h, rows, :] * e1[h]
                w = jnp.where(sel, prod, zero) if w is None else jnp.where(sel, w + prod, w)
            lo = il * nk + m * BF16_ROWS
            p_ref[lo:lo + BF16_ROWS, :] = w * act[lo:lo + BF16_ROWS, :]
    acc_ref[...] += _dot(vt_ref[...], p_ref[...])

    @pl.when(j == pl.num_programs(1) - 1)
    def _():
        y = x1_ref[...] + acc_ref[...].T
        ms = jnp.mean(y * y, axis=-1, keepdims=True)
        o_ref[...] = y * lax.rsqrt(ms + NORM_EPS) * fnw_ref[...]


def _peer_experts(h2t, u_bf, vt_bf, e1, n1, e2, r2, x1, fnw, tb=512, eb=2048):
    t, d = x1.shape
    ne = u_bf.shape[0]
    nh, nk = PEER_HEADS, PEER_NKEYS
    i1b = eb // nk
    return pl.pallas_call(
        functools.partial(_experts_body, i1_per_blk=i1b),
        grid=(t // tb, ne // eb),
        in_specs=[
            pl.BlockSpec((d, tb), lambda i, j: (0, i)),
            pl.BlockSpec((eb, d), lambda i, j: (j, 0)),
            pl.BlockSpec((d, eb), lambda i, j: (0, j)),
            pl.BlockSpec((nh, i1b, tb), lambda i, j: (0, j, i)),
            pl.BlockSpec((nh, i1b, tb), lambda i, j: (0, j, i)),
            pl.BlockSpec((nh, nk, tb), lambda i, j: (0, 0, i)),
            pl.BlockSpec((nh, nk, tb), lambda i, j: (0, 0, i)),
            pl.BlockSpec((tb, d), lambda i, j: (i, 0)),
            pl.BlockSpec((1, d), lambda i, j: (0, 0)),
        ],
        out_specs=pl.BlockSpec((tb, d), lambda i, j: (i, 0)),
        out_shape=jax.ShapeDtypeStruct((t, d), F32),
        scratch_shapes=[pltpu.VMEM((d, tb), F32), pltpu.VMEM((eb, tb), BF16)],
        compiler_params=_params(("parallel", "arbitrary")),
        name="peer_experts",
    )(h2t, u_bf, vt_bf, e1, n1, e2, r2, x1, fnw)


def _rope_tables(s):
    half = RET_DK // 2
    inv = ROPE_BASE ** (-jnp.arange(half, dtype=F32) / half)
    ang = jnp.arange(s, dtype=F32)[:, None] * inv[None, :]
    return jnp.cos(ang), jnp.sin(ang)


def kernel(x, norm_mix_w, w_in, hg_lower_bounds, hg_norm_w, w_branch_hg, w_branch_ret, w_out, norm_ffn_w,
           peer_w_q, peer_sub_keys, expert_u, expert_v, final_norm_w):
    b, s, d = x.shape
    t = b * s
    x2 = x.reshape(t, d)

    proj = _inproj(x2, norm_mix_w[0:1], w_in[0].astype(BF16), BF16)
    proj3 = proj.reshape(b, s, IN_WIDTH)

    o_hg = _hgrn2(proj3, hg_lower_bounds, hg_norm_w[0:1])

    cos_t, sin_t = _rope_tables(s)
    log_gamma = jnp.log(1.0 - jnp.exp2(-5.0 - jnp.arange(RET_HEADS, dtype=F32)))
    log_gamma = jnp.broadcast_to(log_gamma[:, None, None], (RET_HEADS, 1, RET_DV))
    o_ret = _retention(proj3, cos_t, sin_t, log_gamma)

    x1, h2t, q = _merge(
        o_hg.reshape(t, -1), o_ret.reshape(t, -1), proj, x2,
        w_branch_hg[0].astype(BF16), w_branch_ret[0].astype(BF16), w_out[0].astype(BF16),
        norm_ffn_w[0:1], peer_w_q[0].astype(BF16))

    e1, n1, e2, r2 = _peer_topk(q, peer_sub_keys[0].astype(BF16))

    out = _peer_experts(h2t, expert_u[0].astype(BF16), expert_v[0].T.astype(BF16), e1, n1, e2, r2, x1,
                        final_norm_w[None, :])
    return out.reshape(b, s, d)
```

```python
import functools

import jax
import jax.numpy as jnp
from jax import lax
from jax.experimental import pallas as pl
from jax.experimental.pallas import tpu as pltpu

F32 = jnp.float32
BF16 = jnp.bfloat16

D_MODEL = 1024
HG_HEADS = 8
HG_DK = 128
HG_DV = 128
RET_HEADS = 4
RET_DK = 256
RET_DV = 512
RET_CHUNK = 128
ROPE_BASE = 10000.0
PEER_HEADS = 8
PEER_NKEYS = 128
PEER_HALF = 128
PEER_TOPK = 16
NORM_EPS = 1e-6
IN_WIDTH = 12288

VMEM_LIMIT_BYTES = 56 * 1024 * 1024

HG_CHUNK = 64
HG_SUB = 16
NOT_SELECTED_RANK = 99.0
BF16_ROWS = 16
LANES = 128
SUBLANES = 8
LOG2E = 1.4426950408889634


def _sigmoid(x):
    return 1.0 / (1.0 + jnp.exp(-x))


def _dot(a, b):
    return jnp.dot(a, b, preferred_element_type=F32)


def _dot_nt(a, b):
    return lax.dot_general(a, b, (((1,), (1,)), ((), ())), preferred_element_type=F32)


def _dot_tn(a, b):
    return lax.dot_general(a, b, (((0,), (0,)), ((), ())), preferred_element_type=F32)


def _params(sem):
    return pltpu.CompilerParams(dimension_semantics=sem, vmem_limit_bytes=VMEM_LIMIT_BYTES)


def _inproj_body(x_ref, nw_ref, w_ref, o_ref, h_scr):
    @pl.when(pl.program_id(1) == 0)
    def _():
        x = x_ref[...]
        ms = jnp.mean(x * x, axis=-1, keepdims=True)
        h_scr[...] = (x * lax.rsqrt(ms + NORM_EPS) * nw_ref[...]).astype(BF16)

    o_ref[...] = _dot(h_scr[...], w_ref[...]).astype(o_ref.dtype)


def _inproj(x2, norm_w, w_bf, out_dtype, tm=1024, tn=1536):
    t, d = x2.shape
    n = w_bf.shape[1]
    return pl.pallas_call(
        _inproj_body,
        grid=(t // tm, n // tn),
        in_specs=[
            pl.BlockSpec((tm, d), lambda i, j: (i, 0)),
            pl.BlockSpec((1, d), lambda i, j: (0, 0)),
            pl.BlockSpec((d, tn), lambda i, j: (0, j)),
        ],
        out_specs=pl.BlockSpec((tm, tn), lambda i, j: (i, j)),
        out_shape=jax.ShapeDtypeStruct((t, n), out_dtype),
        scratch_shapes=[pltpu.VMEM((tm, d), BF16)],
        compiler_params=_params(("parallel", "arbitrary")),
        name="inproj",
    )(x2, norm_w, w_bf)


def _rows4(vals, n):
    return jnp.concatenate([jnp.broadcast_to(v, (HG_SUB, n)) for v in vals], axis=0)


def _hgrn2_body(q_ref, f_ref, i_ref, g_ref, lbp_ref, nw_ref, o_ref, st_ref, *, n_chunks, heads):
    c, sub, dk = HG_CHUNK, HG_SUB, HG_DK

    @pl.when(pl.program_id(2) == 0)
    def _():
        st_ref[...] = jnp.zeros_like(st_ref)

    lbp = lbp_ref[...]
    mx = jnp.max(lbp, axis=0, keepdims=True)
    el = jnp.exp(lbp - mx)
    lb_all = el[0:1, :] / (el[0:1, :] + el[1:2, :])
    nw_all = nw_ref[...]

    rin = lax.broadcasted_iota(jnp.int32, (c, dk), 0) & (sub - 1)
    r8 = lax.broadcasted_iota(jnp.int32, (SUBLANES, dk), 0)
    lane_c = lax.broadcasted_iota(jnp.int32, (SUBLANES, c), 1)
    sub_shift = sub.bit_length() - 1
    rb = lax.broadcasted_iota(jnp.int32, (c, c), 0) >> sub_shift
    cb = lax.broadcasted_iota(jnp.int32, (c, c), 1) >> sub_shift
    m16 = ((rb & 1) == 1) & (cb == rb - 1)
    m32 = (rb >= 2) & (cb < 2)
    one = jnp.ones((1, dk), F32)

    def head_chunk(hh, sl):
        hc = slice(hh * dk, (hh + 1) * dk)
        lb, nw = lb_all[:, hc], nw_all[:, hc]
        q = q_ref[sl, hc].astype(F32) * (HG_DK ** -0.5)
        fl = f_ref[sl, hc].astype(F32)
        v = i_ref[sl, hc].astype(F32)
        f = lb + (1.0 - lb) * _sigmoid(fl)
        k = 1.0 - f
        lf = jnp.log(f)

        bl = lf
        for sh in (1, 2, 4, 8):
            bl = bl + jnp.where(rin >= sh, pltpu.roll(bl, sh, 0), 0.0)
        g = [bl[sub * j + sub - 1:sub * j + sub, :] for j in range(4)]
        eg = [jnp.exp(x) for x in g]

        q16 = q * jnp.exp(bl)
        k16 = k * jnp.exp(_rows4(g, dk) - bl)
        q32 = q16 * _rows4([one, one, one, eg[2]], dk)
        k32 = k16 * _rows4([eg[1], one, one, one], dk)
        qin = q16 * _rows4([one, eg[0], eg[0] * eg[1], eg[0] * eg[1] * eg[2]], dk)
        kst = k16 * _rows4([eg[1] * eg[2] * eg[3], eg[2] * eg[3], eg[3], one], dk)

        s16 = _dot_nt(q16.astype(BF16), k16.astype(BF16))
        s32 = _dot_nt(q32.astype(BF16), k32.astype(BF16))
        a = jnp.where(m16, s16, 0.0) + jnp.where(m32, s32, 0.0)

        bl2 = bl * LOG2E
        tiles = []
        for j in range(c // sub):
            lo = sub * j
            for hf in range(sub // SUBLANES):
                r0 = lo + SUBLANES * hf
                b_t, q_t = bl2[r0:r0 + SUBLANES], q[r0:r0 + SUBLANES]
                tile = a[r0:r0 + SUBLANES, :]
                for s in range(SUBLANES * (hf + 1)):
                    d = b_t - bl2[lo + s:lo + s + 1, :]
                    if s > SUBLANES * hf:
                        d = jnp.where(r8 >= s - SUBLANES * hf, d, -jnp.inf)
                    p = q_t * k[lo + s:lo + s + 1, :] * jnp.exp2(d)
                    tile = jnp.where(lane_c == lo + s, jnp.sum(p, axis=-1, keepdims=True), tile)
                tiles.append(tile)
        a = jnp.concatenate(tiles, axis=0)

        vb = v.astype(BF16)
        st = st_ref[hh]
        o = _dot(a.astype(BF16), vb) + _dot_nt(qin.astype(BF16), st.astype(BF16))

        st_ref[hh] = st * (eg[0] * eg[1] * eg[2] * eg[3]) + _dot_tn(vb, kst.astype(BF16))

        ms = jnp.mean(o * o, axis=-1, keepdims=True)
        gt = g_ref[sl, hc].astype(F32)
        o_ref[sl, hc] = (o * lax.rsqrt(ms + NORM_EPS) * nw * (gt * _sigmoid(gt))).astype(o_ref.dtype)

    def chunk(ci, carry):
        sl = pl.ds(pl.multiple_of(ci * c, c), c)
        for hh in range(heads):
            head_chunk(hh, sl)
        return carry

    lax.fori_loop(0, n_chunks, chunk, 0)


def _hgrn2(proj3, lb_logits, norm_w, ts=512, heads_per_step=8):
    b, s, _ = proj3.shape
    nh, hps = HG_HEADS, heads_per_step
    w = hps * HG_DK
    ng = nh // hps

    def col(off):
        return pl.BlockSpec((None, ts, w), lambda bi, h, si, off=off: (bi, si, off + h))

    return pl.pallas_call(
        functools.partial(_hgrn2_body, n_chunks=ts // HG_CHUNK, heads=hps),
        grid=(b, ng, s // ts),
        in_specs=[
            col(0), col(ng), col(2 * ng), col(3 * ng),
            pl.BlockSpec((2, w), lambda bi, h, si: (0, h)),
            pl.BlockSpec((1, w), lambda bi, h, si: (0, h)),
        ],
        out_specs=pl.BlockSpec((None, ts, w), lambda bi, h, si: (bi, si, h)),
        out_shape=jax.ShapeDtypeStruct((b, s, nh * HG_DV), BF16),
        scratch_shapes=[pltpu.VMEM((hps, HG_DV, HG_DK), F32)],
        compiler_params=_params(("parallel", "parallel", "arbitrary")),
        name="hgrn2",
    )(proj3, proj3, proj3, proj3, lb_logits, norm_w)


def _ret_body(q_ref, k_ref, v_ref, g_ref, cos_ref, sin_ref, lg_ref, o_ref, st_ref, dm_ref, qd_ref, kd_ref,
              *, n_chunks):
    c, half, nh = RET_CHUNK, RET_DK // 2, RET_HEADS

    @pl.when(pl.program_id(1) == 0)
    def _():
        st_ref[...] = jnp.zeros_like(st_ref)
        ri = lax.broadcasted_iota(jnp.int32, (c, half), 0).astype(F32)
        ci = lax.broadcasted_iota(jnp.int32, (c, half), 1).astype(F32)
        rel = ri - ci
        for h in range(nh):
            lg = lg_ref[h][:, :half]
            dm_ref[h] = jnp.where(rel >= 0, jnp.exp(lg * jnp.maximum(rel, 0.0)), 0.0)
            qd_ref[h] = jnp.exp(lg * (ri + 1.0))
            kd_ref[h] = jnp.exp(lg * (c - 1.0 - ri))

    def cat(x1, x2):
        return jnp.concatenate([x1, x2], axis=1).astype(BF16)

    def head_chunk(h, sl, cs, sn):
        vc = slice(h * RET_DV, (h + 1) * RET_DV)

        def rot(x_ref):
            x1 = x_ref[sl, h * RET_DK:h * RET_DK + half].astype(F32)
            x2 = x_ref[sl, h * RET_DK + half:(h + 1) * RET_DK].astype(F32)
            return x1 * cs - x2 * sn, x1 * sn + x2 * cs

        q1, q2 = rot(q_ref)
        k1, k2 = rot(k_ref)
        k1 = k1 * (RET_DK ** -0.5)
        k2 = k2 * (RET_DK ** -0.5)
        vb = v_ref[sl, vc].astype(BF16)
        st = st_ref[h]
        qdec, kdec = qd_ref[h], kd_ref[h]
        cdec = jnp.exp(lg_ref[h] * float(c))

        a = _dot_nt(cat(q1, q2), cat(k1, k2)) * dm_ref[h]
        o = _dot(a.astype(BF16), vb) + _dot(cat(q1 * qdec, q2 * qdec), st.astype(BF16))
        st_ref[h] = cdec * st + _dot_tn(cat(k1 * kdec, k2 * kdec), vb)

        ms = jnp.mean(o * o, axis=-1, keepdims=True)
        gt = g_ref[sl, vc].astype(F32)
        o_ref[sl, vc] = (o * lax.rsqrt(ms + NORM_EPS) * (gt * _sigmoid(gt))).astype(o_ref.dtype)

    def chunk(ci_, carry):
        sl = pl.ds(pl.multiple_of(ci_ * c, c), c)
        cs = cos_ref[sl, :]
        sn = sin_ref[sl, :]
        for h in range(nh):
            head_chunk(h, sl, cs, sn)
        return carry

    lax.fori_loop(0, n_chunks, chunk, 0)


def _retention(proj3, cos_t, sin_t, log_gamma, ts=512):
    b, s, _ = proj3.shape
    nh = RET_HEADS
    wq, wv = nh * RET_DK, nh * RET_DV
    q_off = 4 * HG_HEADS * HG_DK
    qb, kb = q_off // wq, q_off // wq + 1
    vb, gb = (q_off + 2 * wq) // wv, (q_off + 2 * wq) // wv + 1
    half = RET_DK // 2

    def col(w, blk):
        return pl.BlockSpec((None, ts, w), lambda bi, si, blk=blk: (bi, si, blk))

    return pl.pallas_call(
        functools.partial(_ret_body, n_chunks=ts // RET_CHUNK),
        grid=(b, s // ts),
        in_specs=[
            col(wq, qb), col(wq, kb), col(wv, vb), col(wv, gb),
            pl.BlockSpec((ts, half), lambda bi, si: (si, 0)),
            pl.BlockSpec((ts, half), lambda bi, si: (si, 0)),
            pl.BlockSpec((nh, 1, RET_DV), lambda bi, si: (0, 0, 0)),
        ],
        out_specs=pl.BlockSpec((None, ts, wv), lambda bi, si: (bi, si, 0)),
        out_shape=jax.ShapeDtypeStruct((b, s, wv), BF16),
        scratch_shapes=[
            pltpu.VMEM((nh, RET_DK, RET_DV), F32),
            pltpu.VMEM((nh, RET_CHUNK, RET_CHUNK), F32),
            pltpu.VMEM((nh, RET_CHUNK, half), F32),
            pltpu.VMEM((nh, RET_CHUNK, half), F32),
        ],
        compiler_params=_params(("parallel", "arbitrary")),
        name="retention",
    )(proj3, proj3, proj3, proj3, cos_t, sin_t, log_gamma)


def _merge_body(ohg_ref, oret_ref, ga_ref, gb_ref, x_ref, wbh_ref, wbr_ref, wo_ref, nfw_ref, wq_ref,
                x1_ref, h2t_ref, q_ref):
    yh = _dot(ohg_ref[...], wbh_ref[...])
    yr = _dot(oret_ref[...], wbr_ref[...])
    m = _sigmoid(ga_ref[...].astype(F32)) * yh + _sigmoid(gb_ref[...].astype(F32)) * yr
    x1 = x_ref[...] + _dot(m.astype(BF16), wo_ref[...])
    x1_ref[...] = x1
    ms = jnp.mean(x1 * x1, axis=-1, keepdims=True)
    h2 = x1 * lax.rsqrt(ms + NORM_EPS) * nfw_ref[...]
    h2t_ref[...] = h2.T.astype(BF16)
    q_ref[...] = _dot(h2.astype(BF16), wq_ref[...]).astype(q_ref.dtype)


def _merge(o_hg, o_ret, proj, x2, wbh, wbr, wo, nfw, wq, tm=512):
    t, d = x2.shape
    ga_blk = (IN_WIDTH - 2 * D_MODEL) // D_MODEL
    nq = wq.shape[1]

    def full(a):
        return pl.BlockSpec(a.shape, lambda i: (0, 0), pipeline_mode=pl.Buffered(1))

    return pl.pallas_call(
        _merge_body,
        grid=(t // tm,),
        in_specs=[
            pl.BlockSpec((tm, o_hg.shape[1]), lambda i: (i, 0)),
            pl.BlockSpec((tm, o_ret.shape[1]), lambda i: (i, 0)),
            pl.BlockSpec((tm, d), lambda i: (i, ga_blk)),
            pl.BlockSpec((tm, d), lambda i: (i, ga_blk + 1)),
            pl.BlockSpec((tm, d), lambda i: (i, 0)),
            full(wbh), full(wbr), full(wo), full(nfw), full(wq),
        ],
        out_specs=[
            pl.BlockSpec((tm, d), lambda i: (i, 0)),
            pl.BlockSpec((d, tm), lambda i: (0, i)),
            pl.BlockSpec((tm, nq), lambda i: (i, 0)),
        ],
        out_shape=[
            jax.ShapeDtypeStruct((t, d), F32),
            jax.ShapeDtypeStruct((d, t), BF16),
            jax.ShapeDtypeStruct((t, nq), BF16),
        ],
        compiler_params=_params(("parallel",)),
        name="merge",
    )(o_hg, o_ret, proj, proj, x2, wbh, wbr, wo, nfw, wq)


_CAND_NB = (16, 8, 5, 4, 3, 2, 2, 2)
_CAND_ROWS = 16 + 8 * 7 + 8


def _extract_top(vals, n_rows, rounds, on_round, break_ties):
    iota = lax.broadcasted_iota(jnp.int32, vals.shape, 0).astype(F32)
    for r in range(rounds):
        m = jnp.max(vals, axis=0, keepdims=True)
        sel = vals == m
        if break_ties:
            idx = jnp.min(jnp.where(sel, iota, float(n_rows)), axis=0, keepdims=True)
            sel = iota == idx
        vals = jnp.where(sel, -jnp.inf, vals)
        on_round(r, m, sel)
    return vals


def _topk_body(q_ref, keys_ref, e1_ref, n1_ref, e2_ref, r2_ref, s_scr, r_scr, top_scr, *, n_groups):
    nk, k = PEER_NKEYS, PEER_TOPK
    lanes = LANES
    qb = q_ref[...].astype(BF16)
    for p in range(2):
        s_scr[p] = _dot_nt(keys_ref[p], qb[:, p * PEER_HALF:(p + 1) * PEER_HALF])
    r_scr[...] = jnp.zeros_like(r_scr)
    top_scr[...] = jnp.zeros_like(top_scr)

    r8 = lax.broadcasted_iota(jnp.int32, (8, lanes), 0)

    def lane_group(g):
        return pl.ds(pl.multiple_of(g * lanes, lanes), lanes)

    def stage1(ln, slot, break_ties):
        excess = jnp.zeros((1, lanes), F32)
        for p in range(2):
            r_scr[slot, p] = jnp.full((nk, lanes), NOT_SELECTED_RANK, F32)

            def on_round(r, m, sel, p=p):
                top_scr[slot, p, r:r + 1, :] = m
                pltpu.store(r_scr.at[slot, p], jnp.full((nk, lanes), float(r), F32), mask=sel)

            left = _extract_top(s_scr[p, :, ln], nk, k, on_round, break_ties)
            removed = jnp.sum(jnp.where(left == -jnp.inf, 1.0, 0.0), axis=0, keepdims=True)
            excess = excess + (removed - float(k))
        return excess

    def stage2(ln, slot, break_ties):
        t1 = top_scr[slot, 0]
        t2 = top_scr[slot, 1]

        pieces = [t2 + t1[0:1, :], t2[0:8] + t1[1:2, :]]
        for a in range(2, 8):
            pieces.append(jnp.where(r8 < _CAND_NB[a], t2[0:8] + t1[a:a + 1, :], -jnp.inf))
        pieces.append(t1[8:16] + t2[0:1, :])
        cand = jnp.concatenate(pieces, axis=0)
        best = []
        left = _extract_top(cand, _CAND_ROWS, k, lambda r, m, sel: best.append(m), break_ties)
        selm = jnp.where((left == -jnp.inf) & (cand > -jnp.inf), 1.0, 0.0)
        z = jnp.ones_like(best[0])
        for m in best[1:]:
            z = z + jnp.exp(m - best[0])
        inv_z = 1.0 / z
        n_rows = [jnp.sum(selm[0:16], axis=0, keepdims=True)]
        for a in range(1, 8):
            lo = 16 + 8 * (a - 1)
            n_rows.append(jnp.sum(selm[lo:lo + 8], axis=0, keepdims=True))
        for i in range(8):
            n_rows.append(selm[_CAND_ROWS - 8 + i:_CAND_ROWS - 7 + i, :])
        total = n_rows[0]
        for x in n_rows[1:]:
            total = total + x

        rank1 = r_scr[slot, 0]
        n1 = jnp.zeros((nk, lanes), F32)
        for a in range(k):
            n1 = jnp.where(rank1 == float(a), n_rows[a], n1)
        n1_ref[:, ln] = n1
        e1_ref[:, ln] = jnp.exp(s_scr[0, :, ln] - t1[0:1, :]) * inv_z
        e2_ref[:, ln] = jnp.exp(s_scr[1, :, ln] - t2[0:1, :]).astype(e2_ref.dtype)
        r2_ref[:, ln] = r_scr[slot, 1].astype(r2_ref.dtype)
        return total - float(k)

    def pipelined(g, worst):
        ex2 = stage2(lane_group(jnp.maximum(g - 1, 0)), (g + 1) % 2, break_ties=False)
        ex1 = stage1(lane_group(jnp.minimum(g, n_groups - 1)), g % 2, break_ties=False)
        worst = jnp.maximum(worst, jnp.where(g >= 1, ex2, 0.0))
        return jnp.maximum(worst, jnp.where(g < n_groups, ex1, 0.0))

    worst = lax.fori_loop(0, n_groups + 1, pipelined, jnp.zeros((1, lanes), F32))

    @pl.when(jnp.max(worst) > 0.0)
    def _():
        def exact(g, carry):
            stage1(lane_group(g), 0, break_ties=True)
            stage2(lane_group(g), 0, break_ties=True)
            return carry

        lax.fori_loop(0, n_groups, exact, 0)


def _peer_topk(q, keys_bf, tb=1024):
    t = q.shape[0]
    nh = PEER_HEADS
    spec_o = pl.BlockSpec((None, PEER_NKEYS, tb), lambda i, h: (h, 0, i))
    shp = jax.ShapeDtypeStruct((nh, PEER_NKEYS, t), F32)
    shp_b = jax.ShapeDtypeStruct((nh, PEER_NKEYS, t), BF16)
    return pl.pallas_call(
        functools.partial(_topk_body, n_groups=tb // LANES),
        grid=(t // tb, nh),
        in_specs=[
            pl.BlockSpec((tb, 2 * PEER_HALF), lambda i, h: (i, h)),
            pl.BlockSpec((None, 2, PEER_NKEYS, PEER_HALF), lambda i, h: (h, 0, 0, 0)),
        ],
        out_specs=[spec_o, spec_o, spec_o, spec_o],
        out_shape=[shp, shp, shp_b, shp_b],
        scratch_shapes=[
            pltpu.VMEM((2, PEER_NKEYS, tb), F32),
            pltpu.VMEM((2, 2, PEER_NKEYS, LANES), F32),
            pltpu.VMEM((2, 2, PEER_TOPK, LANES), F32),
        ],
        compiler_params=_params(("parallel", "parallel")),
        name="peer_topk",
    )(q, keys_bf)


_GELU_K0 = -2.0 * 0.7978845608028654 * 1.4426950408889634
_GELU_K1 = _GELU_K0 * 0.044715


def _gelu_tanh(x):
    return x / (1.0 + jnp.exp2(x * (x * x * _GELU_K1 + _GELU_K0)))


def _experts_body(h2t_ref, u_ref, vt_ref, e1_ref, n1_ref, e2_ref, r2_ref, x1_ref, fnw_ref, o_ref,
                  acc_ref, p_ref, *, i1_per_blk):
    j = pl.program_id(1)
    nk = PEER_NKEYS

    @pl.when(j == 0)
    def _():
        acc_ref[...] = jnp.zeros_like(acc_ref)

    act = _gelu_tanh(_dot(u_ref[...], h2t_ref[...])).astype(BF16)
    tb = act.shape[1]
    zero = jnp.zeros((BF16_ROWS, tb), BF16)
    for il in range(i1_per_blk):
        n1 = [jnp.broadcast_to(n1_ref[h, il:il + 1, :], (BF16_ROWS, tb)).astype(BF16) for h in range(PEER_HEADS)]
        e1 = [jnp.broadcast_to(e1_ref[h, il:il + 1, :], (BF16_ROWS, tb)).astype(BF16) for h in range(PEER_HEADS)]
        for m in range(nk // BF16_ROWS):
            rows = slice(m * BF16_ROWS, (m + 1) * BF16_ROWS)
            w = None
            for h in range(PEER_HEADS):
                sel = r2_ref[h, rows, :] < n1[h]
                prod = e2_ref[h, rows, :] * e1[h]
                w = jnp.where(sel, prod, zero) if w is None else jnp.where(sel, w + prod, w)
            lo = il * nk + m * BF16_ROWS
            p_ref[lo:lo + BF16_ROWS, :] = w * act[lo:lo + BF16_ROWS, :]
    acc_ref[...] += _dot(vt_ref[...], p_ref[...])

    @pl.when(j == pl.num_programs(1) - 1)
    def _():
        y = x1_ref[...] + acc_ref[...].T
        ms = jnp.mean(y * y, axis=-1, keepdims=True)
        o_ref[...] = y * lax.rsqrt(ms + NORM_EPS) * fnw_ref[...]


def _peer_experts(h2t, u_bf, vt_bf, e1, n1, e2, r2, x1, fnw, tb=512, eb=2048):
    t, d = x1.shape
    ne = u_bf.shape[0]
    nh, nk = PEER_HEADS, PEER_NKEYS
    i1b = eb // nk
    return pl.pallas_call(
        functools.partial(_experts_body, i1_per_blk=i1b),
        grid=(t // tb, ne // eb),
        in_specs=[
            pl.BlockSpec((d, tb), lambda i, j: (0, i)),
            pl.BlockSpec((eb, d), lambda i, j: (j, 0)),
            pl.BlockSpec((d, eb), lambda i, j: (0, j)),
            pl.BlockSpec((nh, i1b, tb), lambda i, j: (0, j, i)),
            pl.BlockSpec((nh, i1b, tb), lambda i, j: (0, j, i)),
            pl.BlockSpec((nh, nk, tb), lambda i, j: (0, 0, i)),
            pl.BlockSpec((nh, nk, tb), lambda i, j: (0, 0, i)),
            pl.BlockSpec((tb, d), lambda i, j: (i, 0)),
            pl.BlockSpec((1, d), lambda i, j: (0, 0)),
        ],
        out_specs=pl.BlockSpec((tb, d), lambda i, j: (i, 0)),
        out_shape=jax.ShapeDtypeStruct((t, d), F32),
        scratch_shapes=[pltpu.VMEM((d, tb), F32), pltpu.VMEM((eb, tb), BF16)],
        compiler_params=_params(("parallel", "arbitrary")),
        name="peer_experts",
    )(h2t, u_bf, vt_bf, e1, n1, e2, r2, x1, fnw)


def _rope_tables(s):
    half = RET_DK // 2
    inv = ROPE_BASE ** (-jnp.arange(half, dtype=F32) / half)
    ang = jnp.arange(s, dtype=F32)[:, None] * inv[None, :]
    return jnp.cos(ang), jnp.sin(ang)


def kernel(x, norm_mix_w, w_in, hg_lower_bounds, hg_norm_w, w_branch_hg, w_branch_ret, w_out, norm_ffn_w,
           peer_w_q, peer_sub_keys, expert_u, expert_v, final_norm_w):
    b, s, d = x.shape
    t = b * s
    x2 = x.reshape(t, d)

    proj = _inproj(x2, norm_mix_w[0:1], w_in[0].astype(BF16), BF16)
    proj3 = proj.reshape(b, s, IN_WIDTH)

    o_hg = _hgrn2(proj3, hg_lower_bounds, hg_norm_w[0:1])

    cos_t, sin_t = _rope_tables(s)
    log_gamma = jnp.log(1.0 - jnp.exp2(-5.0 - jnp.arange(RET_HEADS, dtype=F32)))
    log_gamma = jnp.broadcast_to(log_gamma[:, None, None], (RET_HEADS, 1, RET_DV))
    o_ret = _retention(proj3, cos_t, sin_t, log_gamma)

    x1, h2t, q = _merge(
        o_hg.reshape(t, -1), o_ret.reshape(t, -1), proj, x2,
        w_branch_hg[0].astype(BF16), w_branch_ret[0].astype(BF16), w_out[0].astype(BF16),
        norm_ffn_w[0:1], peer_w_q[0].astype(BF16))

    e1, n1, e2, r2 = _peer_topk(q, peer_sub_keys[0].astype(BF16))

    out = _peer_experts(h2t, expert_u[0].astype(BF16), expert_v[0].T.astype(BF16), e1, n1, e2, r2, x1,
                        final_norm_w[None, :])
    return out.reshape(b, s, d)
```

```python
import functools

import jax
import jax.numpy as jnp
from jax import lax
from jax.experimental import pallas as pl
from jax.experimental.pallas import tpu as pltpu

F32 = jnp.float32
BF16 = jnp.bfloat16

D_MODEL = 1024
HG_HEADS = 8
HG_DK = 128
HG_DV = 128
RET_HEADS = 4
RET_DK = 256
RET_DV = 512
RET_CHUNK = 128
ROPE_BASE = 10000.0
PEER_HEADS = 8
PEER_NKEYS = 128
PEER_HALF = 128
PEER_TOPK = 16
NORM_EPS = 1e-6
IN_WIDTH = 12288

VMEM_LIMIT_BYTES = 56 * 1024 * 1024

HG_CHUNK = 64
HG_SUB = 16
NOT_SELECTED_RANK = 99.0
BF16_ROWS = 16
LANES = 128
SUBLANES = 8
LOG2E = 1.4426950408889634


def _sigmoid(x):
    return 1.0 / (1.0 + jnp.exp(-x))


def _dot(a, b):
    return jnp.dot(a, b, preferred_element_type=F32)


def _dot_nt(a, b):
    return lax.dot_general(a, b, (((1,), (1,)), ((), ())), preferred_element_type=F32)


def _dot_tn(a, b):
    return lax.dot_general(a, b, (((0,), (0,)), ((), ())), preferred_element_type=F32)


def _params(sem):
    return pltpu.CompilerParams(dimension_semantics=sem, vmem_limit_bytes=VMEM_LIMIT_BYTES)


def _inproj_body(x_ref, nw_ref, w_ref, o_ref, h_scr):
    @pl.when(pl.program_id(1) == 0)
    def _():
        x = x_ref[...]
        ms = jnp.mean(x * x, axis=-1, keepdims=True)
        h_scr[...] = (x * lax.rsqrt(ms + NORM_EPS) * nw_ref[...]).astype(BF16)

    o_ref[...] = _dot(h_scr[...], w_ref[...]).astype(o_ref.dtype)


def _inproj(x2, norm_w, w_bf, out_dtype, tm=1024, tn=3072):
    t, d = x2.shape
    n = w_bf.shape[1]
    return pl.pallas_call(
        _inproj_body,
        grid=(t // tm, n // tn),
        in_specs=[
            pl.BlockSpec((tm, d), lambda i, j: (i, 0)),
            pl.BlockSpec((1, d), lambda i, j: (0, 0)),
            pl.BlockSpec((d, tn), lambda i, j: (0, j)),
        ],
        out_specs=pl.BlockSpec((tm, tn), lambda i, j: (i, j)),
        out_shape=jax.ShapeDtypeStruct((t, n), out_dtype),
        scratch_shapes=[pltpu.VMEM((tm, d), BF16)],
        compiler_params=_params(("parallel", "arbitrary")),
        name="inproj",
    )(x2, norm_w, w_bf)


def _rows4(vals, n):
    return jnp.concatenate([jnp.broadcast_to(v, (HG_SUB, n)) for v in vals], axis=0)


def _hgrn2_body(q_ref, f_ref, i_ref, g_ref, lbp_ref, nw_ref, o_ref, st_ref, *, n_chunks, heads):
    c, sub, dk = HG_CHUNK, HG_SUB, HG_DK

    @pl.when(pl.program_id(2) == 0)
    def _():
        st_ref[...] = jnp.zeros_like(st_ref)

    lbp = lbp_ref[...]
    mx = jnp.max(lbp, axis=0, keepdims=True)
    el = jnp.exp(lbp - mx)
    lb_all = el[0:1, :] / (el[0:1, :] + el[1:2, :])
    nw_all = nw_ref[...]

    rin = lax.broadcasted_iota(jnp.int32, (c, dk), 0) & (sub - 1)
    r8 = lax.broadcasted_iota(jnp.int32, (SUBLANES, dk), 0)
    lane_c = lax.broadcasted_iota(jnp.int32, (SUBLANES, c), 1)
    sub_shift = sub.bit_length() - 1
    rb = lax.broadcasted_iota(jnp.int32, (c, c), 0) >> sub_shift
    cb = lax.broadcasted_iota(jnp.int32, (c, c), 1) >> sub_shift
    m16 = ((rb & 1) == 1) & (cb == rb - 1)
    m32 = (rb >= 2) & (cb < 2)
    one = jnp.ones((1, dk), F32)

    def head_chunk(hh, sl):
        hc = slice(hh * dk, (hh + 1) * dk)
        lb, nw = lb_all[:, hc], nw_all[:, hc]
        q = q_ref[sl, hc].astype(F32) * (HG_DK ** -0.5)
        fl = f_ref[sl, hc].astype(F32)
        v = i_ref[sl, hc].astype(F32)
        f = lb + (1.0 - lb) * _sigmoid(fl)
        k = 1.0 - f
        lf = jnp.log(f)

        bl = lf
        for sh in (1, 2, 4, 8):
            bl = bl + jnp.where(rin >= sh, pltpu.roll(bl, sh, 0), 0.0)
        g = [bl[sub * j + sub - 1:sub * j + sub, :] for j in range(4)]
        eg = [jnp.exp(x) for x in g]

        q16 = q * jnp.exp(bl)
        k16 = k * jnp.exp(_rows4(g, dk) - bl)
        q32 = q16 * _rows4([one, one, one, eg[2]], dk)
        k32 = k16 * _rows4([eg[1], one, one, one], dk)
        qin = q16 * _rows4([one, eg[0], eg[0] * eg[1], eg[0] * eg[1] * eg[2]], dk)
        kst = k16 * _rows4([eg[1] * eg[2] * eg[3], eg[2] * eg[3], eg[3], one], dk)

        s16 = _dot_nt(q16.astype(BF16), k16.astype(BF16))
        s32 = _dot_nt(q32.astype(BF16), k32.astype(BF16))
        a = jnp.where(m16, s16, 0.0) + jnp.where(m32, s32, 0.0)

        bl2 = bl * LOG2E
        tiles = []
        for j in range(c // sub):
            lo = sub * j
            for hf in range(sub // SUBLANES):
                r0 = lo + SUBLANES * hf
                b_t, q_t = bl2[r0:r0 + SUBLANES], q[r0:r0 + SUBLANES]
                tile = a[r0:r0 + SUBLANES, :]
                for s in range(SUBLANES * (hf + 1)):
                    d = b_t - bl2[lo + s:lo + s + 1, :]
                    if s > SUBLANES * hf:
                        d = jnp.where(r8 >= s - SUBLANES * hf, d, -jnp.inf)
                    p = q_t * k[lo + s:lo + s + 1, :] * jnp.exp2(d)
                    tile = jnp.where(lane_c == lo + s, jnp.sum(p, axis=-1, keepdims=True), tile)
                tiles.append(tile)
        a = jnp.concatenate(tiles, axis=0)

        vb = v.astype(BF16)
        st = st_ref[hh]
        o = _dot(a.astype(BF16), vb) + _dot_nt(qin.astype(BF16), st.astype(BF16))

        st_ref[hh] = st * (eg[0] * eg[1] * eg[2] * eg[3]) + _dot_tn(vb, kst.astype(BF16))

        ms = jnp.mean(o * o, axis=-1, keepdims=True)
        gt = g_ref[sl, hc].astype(F32)
        o_ref[sl, hc] = (o * lax.rsqrt(ms + NORM_EPS) * nw * (gt * _sigmoid(gt))).astype(o_ref.dtype)

    def chunk(ci, carry):
        sl = pl.ds(pl.multiple_of(ci * c, c), c)
        for hh in range(heads):
            head_chunk(hh, sl)
        return carry

    lax.fori_loop(0, n_chunks, chunk, 0)


def _hgrn2(proj3, lb_logits, norm_w, ts=512, heads_per_step=8):
    b, s, _ = proj3.shape
    nh, hps = HG_HEADS, heads_per_step
    w = hps * HG_DK
    ng = nh // hps

    def col(off):
        return pl.BlockSpec((None, ts, w), lambda bi, h, si, off=off: (bi, si, off + h))

    return pl.pallas_call(
        functools.partial(_hgrn2_body, n_chunks=ts // HG_CHUNK, heads=hps),
        grid=(b, ng, s // ts),
        in_specs=[
            col(0), col(ng), col(2 * ng), col(3 * ng),
            pl.BlockSpec((2, w), lambda bi, h, si: (0, h)),
            pl.BlockSpec((1, w), lambda bi, h, si: (0, h)),
        ],
        out_specs=pl.BlockSpec((None, ts, w), lambda bi, h, si: (bi, si, h)),
        out_shape=jax.ShapeDtypeStruct((b, s, nh * HG_DV), BF16),
        scratch_shapes=[pltpu.VMEM((hps, HG_DV, HG_DK), F32)],
        compiler_params=_params(("parallel", "parallel", "arbitrary")),
        name="hgrn2",
    )(proj3, proj3, proj3, proj3, lb_logits, norm_w)


def _ret_body(q_ref, k_ref, v_ref, g_ref, cos_ref, sin_ref, lg_ref, o_ref, st_ref, dm_ref, qd_ref, kd_ref,
              *, n_chunks):
    c, half, nh = RET_CHUNK, RET_DK // 2, RET_HEADS

    @pl.when(pl.program_id(1) == 0)
    def _():
        st_ref[...] = jnp.zeros_like(st_ref)
        ri = lax.broadcasted_iota(jnp.int32, (c, half), 0).astype(F32)
        ci = lax.broadcasted_iota(jnp.int32, (c, half), 1).astype(F32)
        rel = ri - ci
        for h in range(nh):
            lg = lg_ref[h][:, :half]
            dm_ref[h] = jnp.where(rel >= 0, jnp.exp(lg * jnp.maximum(rel, 0.0)), 0.0)
            qd_ref[h] = jnp.exp(lg * (ri + 1.0))
            kd_ref[h] = jnp.exp(lg * (c - 1.0 - ri))

    def cat(x1, x2):
        return jnp.concatenate([x1, x2], axis=1).astype(BF16)

    def head_chunk(h, sl, cs, sn):
        vc = slice(h * RET_DV, (h + 1) * RET_DV)

        def rot(x_ref):
            x1 = x_ref[sl, h * RET_DK:h * RET_DK + half].astype(F32)
            x2 = x_ref[sl, h * RET_DK + half:(h + 1) * RET_DK].astype(F32)
            return x1 * cs - x2 * sn, x1 * sn + x2 * cs

        q1, q2 = rot(q_ref)
        k1, k2 = rot(k_ref)
        k1 = k1 * (RET_DK ** -0.5)
        k2 = k2 * (RET_DK ** -0.5)
        vb = v_ref[sl, vc].astype(BF16)
        st = st_ref[h]
        qdec, kdec = qd_ref[h], kd_ref[h]
        cdec = jnp.exp(lg_ref[h] * float(c))

        a = _dot_nt(cat(q1, q2), cat(k1, k2)) * dm_ref[h]
        o = _dot(a.astype(BF16), vb) + _dot(cat(q1 * qdec, q2 * qdec), st.astype(BF16))
        st_ref[h] = cdec * st + _dot_tn(cat(k1 * kdec, k2 * kdec), vb)

        ms = jnp.mean(o * o, axis=-1, keepdims=True)
        gt = g_ref[sl, vc].astype(F32)
        o_ref[sl, vc] = (o * lax.rsqrt(ms + NORM_EPS) * (gt * _sigmoid(gt))).astype(o_ref.dtype)

    def chunk(ci_, carry):
        sl = pl.ds(pl.multiple_of(ci_ * c, c), c)
        cs = cos_ref[sl, :]
        sn = sin_ref[sl, :]
        for h in range(nh):
            head_chunk(h, sl, cs, sn)
        return carry

    lax.fori_loop(0, n_chunks, chunk, 0)


def _retention(proj3, cos_t, sin_t, log_gamma, ts=512):
    b, s, _ = proj3.shape
    nh = RET_HEADS
    wq, wv = nh * RET_DK, nh * RET_DV
    q_off = 4 * HG_HEADS * HG_DK
    qb, kb = q_off // wq, q_off // wq + 1
    vb, gb = (q_off + 2 * wq) // wv, (q_off + 2 * wq) // wv + 1
    half = RET_DK // 2

    def col(w, blk):
        return pl.BlockSpec((None, ts, w), lambda bi, si, blk=blk: (bi, si, blk))

    return pl.pallas_call(
        functools.partial(_ret_body, n_chunks=ts // RET_CHUNK),
        grid=(b, s // ts),
        in_specs=[
            col(wq, qb), col(wq, kb), col(wv, vb), col(wv, gb),
            pl.BlockSpec((ts, half), lambda bi, si: (si, 0)),
            pl.BlockSpec((ts, half), lambda bi, si: (si, 0)),
            pl.BlockSpec((nh, 1, RET_DV), lambda bi, si: (0, 0, 0)),
        ],
        out_specs=pl.BlockSpec((None, ts, wv), lambda bi, si: (bi, si, 0)),
        out_shape=jax.ShapeDtypeStruct((b, s, wv), BF16),
        scratch_shapes=[
            pltpu.VMEM((nh, RET_DK, RET_DV), F32),
            pltpu.VMEM((nh, RET_CHUNK, RET_CHUNK), F32),
            pltpu.VMEM((nh, RET_CHUNK, half), F32),
            pltpu.VMEM((nh, RET_CHUNK, half), F32),
        ],
        compiler_params=_params(("parallel", "arbitrary")),
        name="retention",
    )(proj3, proj3, proj3, proj3, cos_t, sin_t, log_gamma)


def _merge_body(ohg_ref, oret_ref, ga_ref, gb_ref, x_ref, wbh_ref, wbr_ref, wo_ref, nfw_ref, wq_ref,
                x1_ref, h2t_ref, q_ref):
    yh = _dot(ohg_ref[...], wbh_ref[...])
    yr = _dot(oret_ref[...], wbr_ref[...])
    m = _sigmoid(ga_ref[...].astype(F32)) * yh + _sigmoid(gb_ref[...].astype(F32)) * yr
    x1 = x_ref[...] + _dot(m.astype(BF16), wo_ref[...])
    x1_ref[...] = x1
    ms = jnp.mean(x1 * x1, axis=-1, keepdims=True)
    h2 = x1 * lax.rsqrt(ms + NORM_EPS) * nfw_ref[...]
    h2t_ref[...] = h2.T.astype(BF16)
    q_ref[...] = _dot(h2.astype(BF16), wq_ref[...]).astype(q_ref.dtype)


def _merge(o_hg, o_ret, proj, x2, wbh, wbr, wo, nfw, wq, tm=512):
    t, d = x2.shape
    ga_blk = (IN_WIDTH - 2 * D_MODEL) // D_MODEL
    nq = wq.shape[1]

    def full(a):
        return pl.BlockSpec(a.shape, lambda i: (0, 0), pipeline_mode=pl.Buffered(1))

    return pl.pallas_call(
        _merge_body,
        grid=(t // tm,),
        in_specs=[
            pl.BlockSpec((tm, o_hg.shape[1]), lambda i: (i, 0)),
            pl.BlockSpec((tm, o_ret.shape[1]), lambda i: (i, 0)),
            pl.BlockSpec((tm, d), lambda i: (i, ga_blk)),
            pl.BlockSpec((tm, d), lambda i: (i, ga_blk + 1)),
            pl.BlockSpec((tm, d), lambda i: (i, 0)),
            full(wbh), full(wbr), full(wo), full(nfw), full(wq),
        ],
        out_specs=[
            pl.BlockSpec((tm, d), lambda i: (i, 0)),
            pl.BlockSpec((d, tm), lambda i: (0, i)),
            pl.BlockSpec((tm, nq), lambda i: (i, 0)),
        ],
        out_shape=[
            jax.ShapeDtypeStruct((t, d), F32),
            jax.ShapeDtypeStruct((d, t), BF16),
            jax.ShapeDtypeStruct((t, nq), BF16),
        ],
        compiler_params=_params(("parallel",)),
        name="merge",
    )(o_hg, o_ret, proj, proj, x2, wbh, wbr, wo, nfw, wq)


_CAND_NB = (16, 8, 5, 4, 3, 2, 2, 2)
_CAND_ROWS = 16 + 8 * 7 + 8


def _extract_top(vals, n_rows, rounds, on_round, break_ties):
    iota = lax.broadcasted_iota(jnp.int32, vals.shape, 0).astype(F32)
    for r in range(rounds):
        m = jnp.max(vals, axis=0, keepdims=True)
        sel = vals == m
        if break_ties:
            idx = jnp.min(jnp.where(sel, iota, float(n_rows)), axis=0, keepdims=True)
            sel = iota == idx
        vals = jnp.where(sel, -jnp.inf, vals)
        on_round(r, m, sel)
    return vals


_GELU_K0 = -2.0 * 0.7978845608028654 * 1.4426950408889634
_GELU_K1 = _GELU_K0 * 0.044715


def _gelu_tanh(x):
    return x / (1.0 + jnp.exp2(x * (x * x * _GELU_K1 + _GELU_K0)))


def _scores_body(q_ref, keys_ref, o_ref):
    for h in range(PEER_HEADS):
        for p in range(2):
            c0 = (2 * h + p) * PEER_HALF
            o_ref[h, p] = _dot_nt(keys_ref[h, p], q_ref[:, c0:c0 + PEER_HALF])


def _peer_scores(q, keys_bf, ts=1024):
    t = q.shape[0]
    nh, nk = PEER_HEADS, PEER_NKEYS
    return pl.pallas_call(
        _scores_body,
        grid=(t // ts,),
        in_specs=[
            pl.BlockSpec((ts, q.shape[1]), lambda i: (i, 0)),
            pl.BlockSpec(keys_bf.shape, lambda i: (0, 0, 0, 0)),
        ],
        out_specs=pl.BlockSpec((nh, 2, nk, ts), lambda i: (0, 0, 0, i)),
        out_shape=jax.ShapeDtypeStruct((nh, 2, nk, t), F32),
        compiler_params=_params(("parallel",)),
        name="peer_scores",
    )(q, keys_bf)


def _ffn_body(sc_ref, h2t_ref, u_ref, vt_ref, x1_ref, fnw_ref, o_ref,
              acc_ref, p_ref, e1_scr, n1_scr, e2_scr, r2_scr, ss_scr, rk_scr, top_scr, ex_scr,
              st_e1, st_n1, st_e2, st_r2, *, i1_per_blk, n_groups, n_units):
    s = pl.program_id(0)
    nh, nk, k, lanes = PEER_HEADS, PEER_NKEYS, PEER_TOPK, LANES
    new, old = 0, 1
    cur, nxt = 0, 1
    h2 = jnp.clip(s - 1, 0, n_units - 1) % nh
    je = jnp.clip(s - (nh + 1), 0, n_units - 1) % nh
    r8 = lax.broadcasted_iota(jnp.int32, (8, lanes), 0)

    @pl.when(s == 0)
    def _():
        for r in (ss_scr, rk_scr, top_scr):
            r[...] = jnp.zeros_like(r)

    @pl.when(je == 0)
    def _():
        acc_ref[...] = jnp.zeros_like(acc_ref)

    def stage1(g, break_ties):
        excess = jnp.zeros((1, lanes), F32)
        for p in range(2):
            sc = sc_ref[p, :, g * lanes:(g + 1) * lanes]
            ss_scr[new, g, p] = sc
            rk_scr[new, g, p] = jnp.full((nk, lanes), NOT_SELECTED_RANK, F32)

            def on_round(r, m, sel, p=p):
                top_scr[new, g, p, r:r + 1, :] = m
                pltpu.store(rk_scr.at[new, g, p], jnp.full((nk, lanes), float(r), F32), mask=sel)

            left = _extract_top(sc, nk, k, on_round, break_ties)
            removed = jnp.sum(jnp.where(left == -jnp.inf, 1.0, 0.0), axis=0, keepdims=True)
            excess = excess + (removed - float(k))
        return excess

    def stage2(g, break_ties):
        ln = slice(g * lanes, (g + 1) * lanes)
        t1, t2 = top_scr[old, g, 0], top_scr[old, g, 1]
        pieces = [t2 + t1[0:1, :], t2[0:8] + t1[1:2, :]]
        for a in range(2, 8):
            pieces.append(jnp.where(r8 < _CAND_NB[a], t2[0:8] + t1[a:a + 1, :], -jnp.inf))
        pieces.append(t1[8:16] + t2[0:1, :])
        cand = jnp.concatenate(pieces, axis=0)
        best = []
        left = _extract_top(cand, _CAND_ROWS, k, lambda r, m, sel: best.append(m), break_ties)
        selm = jnp.where((left == -jnp.inf) & (cand > -jnp.inf), 1.0, 0.0)
        z = jnp.ones_like(best[0])
        for m in best[1:]:
            z = z + jnp.exp(m - best[0])
        inv_z = 1.0 / z
        n_rows = [jnp.sum(selm[0:16], axis=0, keepdims=True)]
        for a in range(1, 8):
            lo = 16 + 8 * (a - 1)
            n_rows.append(jnp.sum(selm[lo:lo + 8], axis=0, keepdims=True))
        for r in range(8):
            n_rows.append(selm[_CAND_ROWS - 8 + r:_CAND_ROWS - 7 + r, :])
        total = n_rows[0]
        for x in n_rows[1:]:
            total = total + x

        rank1 = rk_scr[old, g, 0]
        n1 = jnp.zeros((nk, lanes), F32)
        for a in range(k):
            n1 = jnp.where(rank1 == float(a), n_rows[a], n1)
        st_n1[:, ln] = n1
        st_e1[:, ln] = jnp.exp(ss_scr[old, g, 0] - t1[0:1, :]) * inv_z
        st_e2[:, ln] = jnp.exp(ss_scr[old, g, 1] - t2[0:1, :]).astype(st_e2.dtype)
        st_r2[:, ln] = rk_scr[old, g, 1].astype(st_r2.dtype)
        return total - float(k)

    def select(break_ties):
        worst1 = jnp.zeros((1, lanes), F32)
        worst2 = jnp.zeros((1, lanes), F32)
        for g in range(n_groups):
            worst1 = jnp.maximum(worst1, stage1(g, break_ties))
            worst2 = jnp.maximum(worst2, stage2(g, break_ties))
        ex_scr[...] = jnp.maximum(jnp.where(s < n_units, worst1, 0.0),
                                  jnp.where((s >= 1) & (s <= n_units), worst2, 0.0))

    def evaluate_experts():
        act = _gelu_tanh(_dot(u_ref[...], h2t_ref[...])).astype(BF16)
        tb = act.shape[1]
        zero = jnp.zeros((BF16_ROWS, tb), BF16)
        for il in range(i1_per_blk):
            row = pl.ds(je * i1_per_blk + il, 1)
            n1 = [jnp.broadcast_to(n1_scr[cur, h, row, :], (BF16_ROWS, tb)).astype(BF16) for h in range(nh)]
            e1 = [jnp.broadcast_to(e1_scr[cur, h, row, :], (BF16_ROWS, tb)).astype(BF16) for h in range(nh)]
            for m in range(nk // BF16_ROWS):
                rows = slice(m * BF16_ROWS, (m + 1) * BF16_ROWS)
                w = None
                for h in range(nh):
                    sel = r2_scr[cur, h, rows, :] < n1[h]
                    prod = e2_scr[cur, h, rows, :] * e1[h]
                    w = jnp.where(sel, prod, zero) if w is None else jnp.where(sel, w + prod, w)
                lo = il * nk + m * BF16_ROWS
                p_ref[lo:lo + BF16_ROWS, :] = w * act[lo:lo + BF16_ROWS, :]
        acc_ref[...] += _dot(vt_ref[...], p_ref[...])

    @pl.when(s <= nh)
    def _():
        select(False)

    @pl.when((s > nh) & (s <= n_units))
    def _():
        select(False)
        evaluate_experts()

    @pl.when(s > n_units)
    def _():
        ex_scr[...] = jnp.zeros_like(ex_scr)
        evaluate_experts()

    @pl.when(jnp.max(ex_scr[...]) > 0.0)
    def _():
        select(True)

    @pl.when(s <= n_units)
    def _():
        for r in (ss_scr, rk_scr, top_scr):
            r[old] = r[new]
        n1_scr[nxt, h2] = st_n1[...]
        e1_scr[nxt, h2] = st_e1[...]
        e2_scr[nxt, h2] = st_e2[...]
        r2_scr[nxt, h2] = st_r2[...]

    @pl.when((s >= nh) & (s % nh == 0))
    def _():
        for r in (n1_scr, e1_scr, e2_scr, r2_scr):
            r[cur] = r[nxt]

    @pl.when((s > nh) & (je == nh - 1))
    def _():
        y = x1_ref[...] + acc_ref[...].T
        ms = jnp.mean(y * y, axis=-1, keepdims=True)
        o_ref[...] = y * lax.rsqrt(ms + NORM_EPS) * fnw_ref[...]


def _peer_ffn(scores, h2t, u_bf, vt_bf, x1, fnw, tb=512):
    t, d = x1.shape
    ne = u_bf.shape[0]
    nh, nk = PEER_HEADS, PEER_NKEYS
    eb = ne // nh
    ng = tb // LANES
    n_units = (t // tb) * nh

    def unit1(s):
        u = jnp.minimum(s, n_units - 1)
        return u // nh, u % nh

    def eblk(s):
        e = jnp.clip(s - (nh + 1), 0, n_units - 1)
        return e // nh, e % nh

    return pl.pallas_call(
        functools.partial(_ffn_body, i1_per_blk=eb // nk, n_groups=ng, n_units=n_units),
        grid=(n_units + nh + 1,),
        in_specs=[
            pl.BlockSpec((None, 2, nk, tb), lambda s: (unit1(s)[1], 0, 0, unit1(s)[0])),
            pl.BlockSpec((d, tb), lambda s: (0, eblk(s)[0])),
            pl.BlockSpec((eb, d), lambda s: (eblk(s)[1], 0)),
            pl.BlockSpec((d, eb), lambda s: (0, eblk(s)[1])),
            pl.BlockSpec((tb, d), lambda s: (eblk(s)[0], 0)),
            pl.BlockSpec((1, d), lambda s: (0, 0)),
        ],
        out_specs=pl.BlockSpec((tb, d), lambda s: (eblk(s)[0], 0)),
        out_shape=jax.ShapeDtypeStruct((t, d), F32),
        scratch_shapes=[
            pltpu.VMEM((d, tb), F32),
            pltpu.VMEM((eb, tb), BF16),
            pltpu.VMEM((2, nh, nk, tb), F32),
            pltpu.VMEM((2, nh, nk, tb), F32),
            pltpu.VMEM((2, nh, nk, tb), BF16),
            pltpu.VMEM((2, nh, nk, tb), BF16),
            pltpu.VMEM((2, ng, 2, nk, LANES), F32),
            pltpu.VMEM((2, ng, 2, nk, LANES), F32),
            pltpu.VMEM((2, ng, 2, PEER_TOPK, LANES), F32),
            pltpu.VMEM((1, LANES), F32),
            pltpu.VMEM((nk, tb), F32), pltpu.VMEM((nk, tb), F32),
            pltpu.VMEM((nk, tb), BF16), pltpu.VMEM((nk, tb), BF16),
        ],
        compiler_params=_params(("arbitrary",)),
        name="peer_ffn",
    )(scores, h2t, u_bf, vt_bf, x1, fnw)


def _rope_tables(s):
    half = RET_DK // 2
    inv = ROPE_BASE ** (-jnp.arange(half, dtype=F32) / half)
    ang = jnp.arange(s, dtype=F32)[:, None] * inv[None, :]
    return jnp.cos(ang), jnp.sin(ang)


def kernel(x, norm_mix_w, w_in, hg_lower_bounds, hg_norm_w, w_branch_hg, w_branch_ret, w_out, norm_ffn_w,
           peer_w_q, peer_sub_keys, expert_u, expert_v, final_norm_w):
    b, s, d = x.shape
    t = b * s
    x2 = x.reshape(t, d)

    proj = _inproj(x2, norm_mix_w[0:1], w_in[0].astype(BF16), BF16)
    proj3 = proj.reshape(b, s, IN_WIDTH)

    o_hg = _hgrn2(proj3, hg_lower_bounds, hg_norm_w[0:1])

    cos_t, sin_t = _rope_tables(s)
    log_gamma = jnp.log(1.0 - jnp.exp2(-5.0 - jnp.arange(RET_HEADS, dtype=F32)))
    log_gamma = jnp.broadcast_to(log_gamma[:, None, None], (RET_HEADS, 1, RET_DV))
    o_ret = _retention(proj3, cos_t, sin_t, log_gamma)

    x1, h2t, q = _merge(
        o_hg.reshape(t, -1), o_ret.reshape(t, -1), proj, x2,
        w_branch_hg[0].astype(BF16), w_branch_ret[0].astype(BF16), w_out[0].astype(BF16),
        norm_ffn_w[0:1], peer_w_q[0].astype(BF16))

    scores = _peer_scores(q, peer_sub_keys[0].astype(BF16))
    out = _peer_ffn(scores, h2t, expert_u[0].astype(BF16), expert_v[0].T.astype(BF16), x1, final_norm_w[None, :])
    return out.reshape(b, s, d)
```

```python
import functools

import jax
import jax.numpy as jnp
from jax import lax
from jax.experimental import pallas as pl
from jax.experimental.pallas import tpu as pltpu

F32 = jnp.float32
BF16 = jnp.bfloat16

D_MODEL = 1024
HG_HEADS = 8
HG_DK = 128
HG_DV = 128
RET_HEADS = 4
RET_DK = 256
RET_DV = 512
RET_CHUNK = 128
ROPE_BASE = 10000.0
PEER_HEADS = 8
PEER_NKEYS = 128
PEER_HALF = 128
PEER_TOPK = 16
NORM_EPS = 1e-6
IN_WIDTH = 12288

VMEM_LIMIT_BYTES = 56 * 1024 * 1024

HG_CHUNK = 64
HG_SUB = 16
NOT_SELECTED_RANK = 99.0
BF16_ROWS = 16
LANES = 128
SUBLANES = 8
LOG2E = 1.4426950408889634


def _sigmoid(x):
    return 1.0 / (1.0 + jnp.exp(-x))


def _dot(a, b):
    return jnp.dot(a, b, preferred_element_type=F32)


def _dot_nt(a, b):
    return lax.dot_general(a, b, (((1,), (1,)), ((), ())), preferred_element_type=F32)


def _dot_tn(a, b):
    return lax.dot_general(a, b, (((0,), (0,)), ((), ())), preferred_element_type=F32)


def _params(sem):
    return pltpu.CompilerParams(dimension_semantics=sem, vmem_limit_bytes=VMEM_LIMIT_BYTES)


def _inproj_body(x_ref, nw_ref, w_ref, o_ref, h_scr):
    @pl.when(pl.program_id(1) == 0)
    def _():
        x = x_ref[...]
        ms = jnp.mean(x * x, axis=-1, keepdims=True)
        h_scr[...] = (x * lax.rsqrt(ms + NORM_EPS) * nw_ref[...]).astype(BF16)

    o_ref[...] = _dot(h_scr[...], w_ref[...]).astype(o_ref.dtype)


def _inproj(x2, norm_w, w_bf, out_dtype, tm=1024, tn=3072):
    t, d = x2.shape
    n = w_bf.shape[1]
    return pl.pallas_call(
        _inproj_body,
        grid=(t // tm, n // tn),
        in_specs=[
            pl.BlockSpec((tm, d), lambda i, j: (i, 0)),
            pl.BlockSpec((1, d), lambda i, j: (0, 0)),
            pl.BlockSpec((d, tn), lambda i, j: (0, j)),
        ],
        out_specs=pl.BlockSpec((tm, tn), lambda i, j: (i, j)),
        out_shape=jax.ShapeDtypeStruct((t, n), out_dtype),
        scratch_shapes=[pltpu.VMEM((tm, d), BF16)],
        compiler_params=_params(("parallel", "arbitrary")),
        name="inproj",
    )(x2, norm_w, w_bf)


def _rows4(vals, n):
    return jnp.concatenate([jnp.broadcast_to(v, (HG_SUB, n)) for v in vals], axis=0)


def _hgrn2_setup(q_ref, f_ref, i_ref, g_ref, lbp_ref, nw_ref, o_ref, st_ref, *, heads):
    c, sub, dk = HG_CHUNK, HG_SUB, HG_DK

    def init():
        st_ref[...] = jnp.zeros_like(st_ref)

    lbp = lbp_ref[...]
    mx = jnp.max(lbp, axis=0, keepdims=True)
    el = jnp.exp(lbp - mx)
    lb_all = el[0:1, :] / (el[0:1, :] + el[1:2, :])
    nw_all = nw_ref[...]

    rin = lax.broadcasted_iota(jnp.int32, (c, dk), 0) & (sub - 1)
    r8 = lax.broadcasted_iota(jnp.int32, (SUBLANES, dk), 0)
    lane_c = lax.broadcasted_iota(jnp.int32, (SUBLANES, c), 1)
    sub_shift = sub.bit_length() - 1
    rb = lax.broadcasted_iota(jnp.int32, (c, c), 0) >> sub_shift
    cb = lax.broadcasted_iota(jnp.int32, (c, c), 1) >> sub_shift
    m16 = ((rb & 1) == 1) & (cb == rb - 1)
    m32 = (rb >= 2) & (cb < 2)
    one = jnp.ones((1, dk), F32)

    def head_chunk(hh, sl):
        hc = slice(hh * dk, (hh + 1) * dk)
        lb, nw = lb_all[:, hc], nw_all[:, hc]
        q = q_ref[sl, hc].astype(F32) * (HG_DK ** -0.5)
        fl = f_ref[sl, hc].astype(F32)
        v = i_ref[sl, hc].astype(F32)
        f = lb + (1.0 - lb) * _sigmoid(fl)
        k = 1.0 - f
        lf = jnp.log(f)

        bl = lf
        for sh in (1, 2, 4, 8):
            bl = bl + jnp.where(rin >= sh, pltpu.roll(bl, sh, 0), 0.0)
        g = [bl[sub * j + sub - 1:sub * j + sub, :] for j in range(4)]
        eg = [jnp.exp(x) for x in g]

        q16 = q * jnp.exp(bl)
        k16 = k * jnp.exp(_rows4(g, dk) - bl)
        q32 = q16 * _rows4([one, one, one, eg[2]], dk)
        k32 = k16 * _rows4([eg[1], one, one, one], dk)
        qin = q16 * _rows4([one, eg[0], eg[0] * eg[1], eg[0] * eg[1] * eg[2]], dk)
        kst = k16 * _rows4([eg[1] * eg[2] * eg[3], eg[2] * eg[3], eg[3], one], dk)

        s16 = _dot_nt(q16.astype(BF16), k16.astype(BF16))
        s32 = _dot_nt(q32.astype(BF16), k32.astype(BF16))
        a = jnp.where(m16, s16, 0.0) + jnp.where(m32, s32, 0.0)

        bl2 = bl * LOG2E
        tiles = []
        for j in range(c // sub):
            lo = sub * j
            for hf in range(sub // SUBLANES):
                r0 = lo + SUBLANES * hf
                b_t, q_t = bl2[r0:r0 + SUBLANES], q[r0:r0 + SUBLANES]
                tile = a[r0:r0 + SUBLANES, :]
                for s in range(SUBLANES * (hf + 1)):
                    d = b_t - bl2[lo + s:lo + s + 1, :]
                    if s > SUBLANES * hf:
                        d = jnp.where(r8 >= s - SUBLANES * hf, d, -jnp.inf)
                    p = q_t * k[lo + s:lo + s + 1, :] * jnp.exp2(d)
                    tile = jnp.where(lane_c == lo + s, jnp.sum(p, axis=-1, keepdims=True), tile)
                tiles.append(tile)
        a = jnp.concatenate(tiles, axis=0)

        vb = v.astype(BF16)
        st = st_ref[hh]
        o = _dot(a.astype(BF16), vb) + _dot_nt(qin.astype(BF16), st.astype(BF16))

        st_ref[hh] = st * (eg[0] * eg[1] * eg[2] * eg[3]) + _dot_tn(vb, kst.astype(BF16))

        ms = jnp.mean(o * o, axis=-1, keepdims=True)
        gt = g_ref[sl, hc].astype(F32)
        o_ref[sl, hc] = (o * lax.rsqrt(ms + NORM_EPS) * nw * (gt * _sigmoid(gt))).astype(o_ref.dtype)

    def chunk(sl):
        for hh in range(heads):
            head_chunk(hh, sl)

    return init, chunk


def _ret_setup(q_ref, k_ref, v_ref, g_ref, cos_ref, sin_ref, lg_ref, o_ref, st_ref, dm_ref, qd_ref, kd_ref):
    c, half, nh = RET_CHUNK, RET_DK // 2, RET_HEADS

    def init():
        st_ref[...] = jnp.zeros_like(st_ref)
        ri = lax.broadcasted_iota(jnp.int32, (c, half), 0).astype(F32)
        ci = lax.broadcasted_iota(jnp.int32, (c, half), 1).astype(F32)
        rel = ri - ci
        for h in range(nh):
            lg = lg_ref[h][:, :half]
            dm_ref[h] = jnp.where(rel >= 0, jnp.exp(lg * jnp.maximum(rel, 0.0)), 0.0)
            qd_ref[h] = jnp.exp(lg * (ri + 1.0))
            kd_ref[h] = jnp.exp(lg * (c - 1.0 - ri))

    def cat(x1, x2):
        return jnp.concatenate([x1, x2], axis=1).astype(BF16)

    def head_chunk(h, sl, cs, sn):
        vc = slice(h * RET_DV, (h + 1) * RET_DV)

        def rot(x_ref):
            x1 = x_ref[sl, h * RET_DK:h * RET_DK + half].astype(F32)
            x2 = x_ref[sl, h * RET_DK + half:(h + 1) * RET_DK].astype(F32)
            return x1 * cs - x2 * sn, x1 * sn + x2 * cs

        q1, q2 = rot(q_ref)
        k1, k2 = rot(k_ref)
        k1 = k1 * (RET_DK ** -0.5)
        k2 = k2 * (RET_DK ** -0.5)
        vb = v_ref[sl, vc].astype(BF16)
        st = st_ref[h]
        qdec, kdec = qd_ref[h], kd_ref[h]
        cdec = jnp.exp(lg_ref[h] * float(c))

        a = _dot_nt(cat(q1, q2), cat(k1, k2)) * dm_ref[h]
        o = _dot(a.astype(BF16), vb) + _dot(cat(q1 * qdec, q2 * qdec), st.astype(BF16))
        st_ref[h] = cdec * st + _dot_tn(cat(k1 * kdec, k2 * kdec), vb)

        ms = jnp.mean(o * o, axis=-1, keepdims=True)
        gt = g_ref[sl, vc].astype(F32)
        o_ref[sl, vc] = (o * lax.rsqrt(ms + NORM_EPS) * (gt * _sigmoid(gt))).astype(o_ref.dtype)

    def chunk(sl):
        cs = cos_ref[sl, :]
        sn = sin_ref[sl, :]
        for h in range(nh):
            head_chunk(h, sl, cs, sn)

    return init, chunk


def _mix_body(hq_ref, hf_ref, hi_ref, hg_ref, lbp_ref, hnw_ref, rq_ref, rk_ref, rv_ref, rg_ref, cos_ref, sin_ref,
              lg_ref, ohg_ref, oret_ref, hst_ref, rst_ref, dm_ref, qd_ref, kd_ref, *, n_units):
    h_init, h_chunk = _hgrn2_setup(hq_ref, hf_ref, hi_ref, hg_ref, lbp_ref, hnw_ref, ohg_ref, hst_ref,
                                   heads=HG_HEADS)
    r_init, r_chunk = _ret_setup(rq_ref, rk_ref, rv_ref, rg_ref, cos_ref, sin_ref, lg_ref, oret_ref, rst_ref,
                                 dm_ref, qd_ref, kd_ref)

    @pl.when(pl.program_id(1) == 0)
    def _():
        h_init()
        r_init()

    def unit(ui, carry):
        base = ui * RET_CHUNK
        for sub in range(RET_CHUNK // HG_CHUNK):
            h_chunk(pl.ds(pl.multiple_of(base + sub * HG_CHUNK, HG_CHUNK), HG_CHUNK))
        r_chunk(pl.ds(pl.multiple_of(base, RET_CHUNK), RET_CHUNK))
        return carry

    lax.fori_loop(0, n_units, unit, 0)


def _mixers(proj3, lb_logits, hg_norm_w, cos_t, sin_t, log_gamma, ts=512):
    b, s, _ = proj3.shape
    wh = HG_HEADS * HG_DK
    wq, wv = RET_HEADS * RET_DK, RET_HEADS * RET_DV
    q_off = 4 * wh
    qb, kb = q_off // wq, q_off // wq + 1
    vb, gb = (q_off + 2 * wq) // wv, (q_off + 2 * wq) // wv + 1
    half = RET_DK // 2

    def col(w, blk):
        return pl.BlockSpec((None, ts, w), lambda bi, si, blk=blk: (bi, si, blk))

    return pl.pallas_call(
        functools.partial(_mix_body, n_units=ts // RET_CHUNK),
        grid=(b, s // ts),
        in_specs=[
            col(wh, 0), col(wh, 1), col(wh, 2), col(wh, 3),
            pl.BlockSpec((2, wh), lambda bi, si: (0, 0)),
            pl.BlockSpec((1, wh), lambda bi, si: (0, 0)),
            col(wq, qb), col(wq, kb), col(wv, vb), col(wv, gb),
            pl.BlockSpec((ts, half), lambda bi, si: (si, 0)),
            pl.BlockSpec((ts, half), lambda bi, si: (si, 0)),
            pl.BlockSpec((RET_HEADS, 1, RET_DV), lambda bi, si: (0, 0, 0)),
        ],
        out_specs=[
            pl.BlockSpec((None, ts, wh), lambda bi, si: (bi, si, 0)),
            pl.BlockSpec((None, ts, wv), lambda bi, si: (bi, si, 0)),
        ],
        out_shape=[
            jax.ShapeDtypeStruct((b, s, wh), BF16),
            jax.ShapeDtypeStruct((b, s, wv), BF16),
        ],
        scratch_shapes=[
            pltpu.VMEM((HG_HEADS, HG_DV, HG_DK), F32),
            pltpu.VMEM((RET_HEADS, RET_DK, RET_DV), F32),
            pltpu.VMEM((RET_HEADS, RET_CHUNK, RET_CHUNK), F32),
            pltpu.VMEM((RET_HEADS, RET_CHUNK, half), F32),
            pltpu.VMEM((RET_HEADS, RET_CHUNK, half), F32),
        ],
        compiler_params=_params(("parallel", "arbitrary")),
        name="mixers",
    )(proj3, proj3, proj3, proj3, lb_logits, hg_norm_w, proj3, proj3, proj3, proj3, cos_t, sin_t, log_gamma)


def _merge_body(ohg_ref, oret_ref, ga_ref, gb_ref, x_ref, wbh_ref, wbr_ref, wo_ref, nfw_ref, wq_ref,
                x1_ref, h2t_ref, q_ref):
    yh = _dot(ohg_ref[...], wbh_ref[...])
    yr = _dot(oret_ref[...], wbr_ref[...])
    m = _sigmoid(ga_ref[...].astype(F32)) * yh + _sigmoid(gb_ref[...].astype(F32)) * yr
    x1 = x_ref[...] + _dot(m.astype(BF16), wo_ref[...])
    x1_ref[...] = x1
    ms = jnp.mean(x1 * x1, axis=-1, keepdims=True)
    h2 = x1 * lax.rsqrt(ms + NORM_EPS) * nfw_ref[...]
    h2t_ref[...] = h2.T.astype(BF16)
    q_ref[...] = _dot(h2.astype(BF16), wq_ref[...]).astype(q_ref.dtype)


def _merge(o_hg, o_ret, proj, x2, wbh, wbr, wo, nfw, wq, tm=512):
    t, d = x2.shape
    ga_blk = (IN_WIDTH - 2 * D_MODEL) // D_MODEL
    nq = wq.shape[1]

    def full(a):
        return pl.BlockSpec(a.shape, lambda i: (0, 0), pipeline_mode=pl.Buffered(1))

    return pl.pallas_call(
        _merge_body,
        grid=(t // tm,),
        in_specs=[
            pl.BlockSpec((tm, o_hg.shape[1]), lambda i: (i, 0)),
            pl.BlockSpec((tm, o_ret.shape[1]), lambda i: (i, 0)),
            pl.BlockSpec((tm, d), lambda i: (i, ga_blk)),
            pl.BlockSpec((tm, d), lambda i: (i, ga_blk + 1)),
            pl.BlockSpec((tm, d), lambda i: (i, 0)),
            full(wbh), full(wbr), full(wo), full(nfw), full(wq),
        ],
        out_specs=[
            pl.BlockSpec((tm, d), lambda i: (i, 0)),
            pl.BlockSpec((d, tm), lambda i: (0, i)),
            pl.BlockSpec((tm, nq), lambda i: (i, 0)),
        ],
        out_shape=[
            jax.ShapeDtypeStruct((t, d), F32),
            jax.ShapeDtypeStruct((d, t), BF16),
            jax.ShapeDtypeStruct((t, nq), BF16),
        ],
        compiler_params=_params(("parallel",)),
        name="merge",
    )(o_hg, o_ret, proj, proj, x2, wbh, wbr, wo, nfw, wq)


_CAND_NB = (16, 8, 5, 4, 3, 2, 2, 2)
_CAND_ROWS = 16 + 8 * 7 + 8


def _extract_top(vals, n_rows, rounds, on_round, break_ties):
    iota = lax.broadcasted_iota(jnp.int32, vals.shape, 0).astype(F32)
    for r in range(rounds):
        m = jnp.max(vals, axis=0, keepdims=True)
        sel = vals == m
        if break_ties:
            idx = jnp.min(jnp.where(sel, iota, float(n_rows)), axis=0, keepdims=True)
            sel = iota == idx
        vals = jnp.where(sel, -jnp.inf, vals)
        on_round(r, m, sel)
    return vals


_GELU_K0 = -2.0 * 0.7978845608028654 * 1.4426950408889634
_GELU_K1 = _GELU_K0 * 0.044715


def _gelu_tanh(x):
    return x / (1.0 + jnp.exp2(x * (x * x * _GELU_K1 + _GELU_K0)))


def _scores_body(q_ref, keys_ref, o_ref):
    for h in range(PEER_HEADS):
        for p in range(2):
            c0 = (2 * h + p) * PEER_HALF
            o_ref[h, p] = _dot_nt(keys_ref[h, p], q_ref[:, c0:c0 + PEER_HALF])


def _peer_scores(q, keys_bf, ts=1024):
    t = q.shape[0]
    nh, nk = PEER_HEADS, PEER_NKEYS
    return pl.pallas_call(
        _scores_body,
        grid=(t // ts,),
        in_specs=[
            pl.BlockSpec((ts, q.shape[1]), lambda i: (i, 0)),
            pl.BlockSpec(keys_bf.shape, lambda i: (0, 0, 0, 0)),
        ],
        out_specs=pl.BlockSpec((nh, 2, nk, ts), lambda i: (0, 0, 0, i)),
        out_shape=jax.ShapeDtypeStruct((nh, 2, nk, t), F32),
        compiler_params=_params(("parallel",)),
        name="peer_scores",
    )(q, keys_bf)


def _ffn_body(sc_ref, h2t_ref, u_ref, vt_ref, x1_ref, fnw_ref, o_ref,
              acc_ref, p_ref, e1_scr, n1_scr, e2_scr, r2_scr, ss_scr, rk_scr, top_scr, ex_scr,
              st_e1, st_n1, st_e2, st_r2, *, i1_per_blk, n_groups, n_units):
    s = pl.program_id(0)
    nh, nk, k, lanes = PEER_HEADS, PEER_NKEYS, PEER_TOPK, LANES
    new, old = 0, 1
    cur, nxt = 0, 1
    h2 = jnp.clip(s - 1, 0, n_units - 1) % nh
    je = jnp.clip(s - (nh + 1), 0, n_units - 1) % nh
    r8 = lax.broadcasted_iota(jnp.int32, (8, lanes), 0)

    @pl.when(s == 0)
    def _():
        for r in (ss_scr, rk_scr, top_scr):
            r[...] = jnp.zeros_like(r)

    @pl.when(je == 0)
    def _():
        acc_ref[...] = jnp.zeros_like(acc_ref)

    def stage1(g, break_ties):
        excess = jnp.zeros((1, lanes), F32)
        for p in range(2):
            sc = sc_ref[p, :, g * lanes:(g + 1) * lanes]
            ss_scr[new, g, p] = sc
            rk_scr[new, g, p] = jnp.full((nk, lanes), NOT_SELECTED_RANK, F32)

            def on_round(r, m, sel, p=p):
                top_scr[new, g, p, r:r + 1, :] = m
                pltpu.store(rk_scr.at[new, g, p], jnp.full((nk, lanes), float(r), F32), mask=sel)

            left = _extract_top(sc, nk, k, on_round, break_ties)
            removed = jnp.sum(jnp.where(left == -jnp.inf, 1.0, 0.0), axis=0, keepdims=True)
            excess = excess + (removed - float(k))
        return excess

    def stage2(g, break_ties):
        ln = slice(g * lanes, (g + 1) * lanes)
        t1, t2 = top_scr[old, g, 0], top_scr[old, g, 1]
        pieces = [t2 + t1[0:1, :], t2[0:8] + t1[1:2, :]]
        for a in range(2, 8):
            pieces.append(jnp.where(r8 < _CAND_NB[a], t2[0:8] + t1[a:a + 1, :], -jnp.inf))
        pieces.append(t1[8:16] + t2[0:1, :])
        cand = jnp.concatenate(pieces, axis=0)
        best = []
        left = _extract_top(cand, _CAND_ROWS, k, lambda r, m, sel: best.append(m), break_ties)
        selm = jnp.where((left == -jnp.inf) & (cand > -jnp.inf), 1.0, 0.0)
        z = jnp.ones_like(best[0])
        for m in best[1:]:
            z = z + jnp.exp(m - best[0])
        inv_z = 1.0 / z
        n_rows = [jnp.sum(selm[0:16], axis=0, keepdims=True)]
        for a in range(1, 8):
            lo = 16 + 8 * (a - 1)
            n_rows.append(jnp.sum(selm[lo:lo + 8], axis=0, keepdims=True))
        for r in range(8):
            n_rows.append(selm[_CAND_ROWS - 8 + r:_CAND_ROWS - 7 + r, :])
        total = n_rows[0]
        for x in n_rows[1:]:
            total = total + x

        rank1 = rk_scr[old, g, 0]
        n1 = jnp.zeros((nk, lanes), F32)
        for a in range(k):
            n1 = jnp.where(rank1 == float(a), n_rows[a], n1)
        st_n1[:, ln] = n1
        st_e1[:, ln] = jnp.exp(ss_scr[old, g, 0] - t1[0:1, :]) * inv_z
        st_e2[:, ln] = jnp.exp(ss_scr[old, g, 1] - t2[0:1, :]).astype(st_e2.dtype)
        st_r2[:, ln] = rk_scr[old, g, 1].astype(st_r2.dtype)
        return total - float(k)

    def select(break_ties):
        worst1 = jnp.zeros((1, lanes), F32)
        worst2 = jnp.zeros((1, lanes), F32)
        for g in range(n_groups):
            worst1 = jnp.maximum(worst1, stage1(g, break_ties))
            worst2 = jnp.maximum(worst2, stage2(g, break_ties))
        ex_scr[...] = jnp.maximum(jnp.where(s < n_units, worst1, 0.0),
                                  jnp.where((s >= 1) & (s <= n_units), worst2, 0.0))

    def evaluate_experts():
        act = _gelu_tanh(_dot(u_ref[...], h2t_ref[...])).astype(BF16)
        tb = act.shape[1]
        zero = jnp.zeros((BF16_ROWS, tb), BF16)
        for il in range(i1_per_blk):
            row = pl.ds(je * i1_per_blk + il, 1)
            n1 = [jnp.broadcast_to(n1_scr[cur, h, row, :], (BF16_ROWS, tb)).astype(BF16) for h in range(nh)]
            e1 = [jnp.broadcast_to(e1_scr[cur, h, row, :], (BF16_ROWS, tb)).astype(BF16) for h in range(nh)]
            for m in range(nk // BF16_ROWS):
                rows = slice(m * BF16_ROWS, (m + 1) * BF16_ROWS)
                w = None
                for h in range(nh):
                    sel = r2_scr[cur, h, rows, :] < n1[h]
                    prod = e2_scr[cur, h, rows, :] * e1[h]
                    w = jnp.where(sel, prod, zero) if w is None else jnp.where(sel, w + prod, w)
                lo = il * nk + m * BF16_ROWS
                p_ref[lo:lo + BF16_ROWS, :] = w * act[lo:lo + BF16_ROWS, :]
        acc_ref[...] += _dot(vt_ref[...], p_ref[...])

    @pl.when(s <= nh)
    def _():
        select(False)

    @pl.when((s > nh) & (s <= n_units))
    def _():
        select(False)
        evaluate_experts()

    @pl.when(s > n_units)
    def _():
        ex_scr[...] = jnp.zeros_like(ex_scr)
        evaluate_experts()

    @pl.when(jnp.max(ex_scr[...]) > 0.0)
    def _():
        select(True)

    @pl.when(s <= n_units)
    def _():
        for r in (ss_scr, rk_scr, top_scr):
            r[old] = r[new]
        n1_scr[nxt, h2] = st_n1[...]
        e1_scr[nxt, h2] = st_e1[...]
        e2_scr[nxt, h2] = st_e2[...]
        r2_scr[nxt, h2] = st_r2[...]

    @pl.when((s >= nh) & (s % nh == 0))
    def _():
        for r in (n1_scr, e1_scr, e2_scr, r2_scr):
            r[cur] = r[nxt]

    @pl.when((s > nh) & (je == nh - 1))
    def _():
        y = x1_ref[...] + acc_ref[...].T
        ms = jnp.mean(y * y, axis=-1, keepdims=True)
        o_ref[...] = y * lax.rsqrt(ms + NORM_EPS) * fnw_ref[...]


def _peer_ffn(scores, h2t, u_bf, vt_bf, x1, fnw, tb=512):
    t, d = x1.shape
    ne = u_bf.shape[0]
    nh, nk = PEER_HEADS, PEER_NKEYS
    eb = ne // nh
    ng = tb // LANES
    n_units = (t // tb) * nh

    def unit1(s):
        u = jnp.minimum(s, n_units - 1)
        return u // nh, u % nh

    def eblk(s):
        e = jnp.clip(s - (nh + 1), 0, n_units - 1)
        return e // nh, e % nh

    return pl.pallas_call(
        functools.partial(_ffn_body, i1_per_blk=eb // nk, n_groups=ng, n_units=n_units),
        grid=(n_units + nh + 1,),
        in_specs=[
            pl.BlockSpec((None, 2, nk, tb), lambda s: (unit1(s)[1], 0, 0, unit1(s)[0])),
            pl.BlockSpec((d, tb), lambda s: (0, eblk(s)[0])),
            pl.BlockSpec((eb, d), lambda s: (eblk(s)[1], 0)),
            pl.BlockSpec((d, eb), lambda s: (0, eblk(s)[1])),
            pl.BlockSpec((tb, d), lambda s: (eblk(s)[0], 0)),
            pl.BlockSpec((1, d), lambda s: (0, 0)),
        ],
        out_specs=pl.BlockSpec((tb, d), lambda s: (eblk(s)[0], 0)),
        out_shape=jax.ShapeDtypeStruct((t, d), F32),
        scratch_shapes=[
            pltpu.VMEM((d, tb), F32),
            pltpu.VMEM((eb, tb), BF16),
            pltpu.VMEM((2, nh, nk, tb), F32),
            pltpu.VMEM((2, nh, nk, tb), F32),
            pltpu.VMEM((2, nh, nk, tb), BF16),
            pltpu.VMEM((2, nh, nk, tb), BF16),
            pltpu.VMEM((2, ng, 2, nk, LANES), F32),
            pltpu.VMEM((2, ng, 2, nk, LANES), F32),
            pltpu.VMEM((2, ng, 2, PEER_TOPK, LANES), F32),
            pltpu.VMEM((1, LANES), F32),
            pltpu.VMEM((nk, tb), F32), pltpu.VMEM((nk, tb), F32),
            pltpu.VMEM((nk, tb), BF16), pltpu.VMEM((nk, tb), BF16),
        ],
        compiler_params=_params(("arbitrary",)),
        name="peer_ffn",
    )(scores, h2t, u_bf, vt_bf, x1, fnw)


def _rope_tables(s):
    half = RET_DK // 2
    inv = ROPE_BASE ** (-jnp.arange(half, dtype=F32) / half)
    ang = jnp.arange(s, dtype=F32)[:, None] * inv[None, :]
    return jnp.cos(ang), jnp.sin(ang)


def kernel(x, norm_mix_w, w_in, hg_lower_bounds, hg_norm_w, w_branch_hg, w_branch_ret, w_out, norm_ffn_w,
           peer_w_q, peer_sub_keys, expert_u, expert_v, final_norm_w):
    b, s, d = x.shape
    t = b * s
    x2 = x.reshape(t, d)

    proj = _inproj(x2, norm_mix_w[0:1], w_in[0].astype(BF16), BF16)
    proj3 = proj.reshape(b, s, IN_WIDTH)

    cos_t, sin_t = _rope_tables(s)
    log_gamma = jnp.log(1.0 - jnp.exp2(-5.0 - jnp.arange(RET_HEADS, dtype=F32)))
    log_gamma = jnp.broadcast_to(log_gamma[:, None, None], (RET_HEADS, 1, RET_DV))
    o_hg, o_ret = _mixers(proj3, hg_lower_bounds, hg_norm_w[0:1], cos_t, sin_t, log_gamma)

    x1, h2t, q = _merge(
        o_hg.reshape(t, -1), o_ret.reshape(t, -1), proj, x2,
        w_branch_hg[0].astype(BF16), w_branch_ret[0].astype(BF16), w_out[0].astype(BF16),
        norm_ffn_w[0:1], peer_w_q[0].astype(BF16))

    scores = _peer_scores(q, peer_sub_keys[0].astype(BF16))
    out = _peer_ffn(scores, h2t, expert_u[0].astype(BF16), expert_v[0].T.astype(BF16), x1, final_norm_w[None, :])
    return out.reshape(b, s, d)
```

```python
import functools

import jax
import jax.numpy as jnp
from jax import lax
from jax.experimental import pallas as pl
from jax.experimental.pallas import tpu as pltpu

F32 = jnp.float32
BF16 = jnp.bfloat16

D_MODEL = 1024
HG_HEADS = 8
HG_DK = 128
HG_DV = 128
RET_HEADS = 4
RET_DK = 256
RET_DV = 512
RET_CHUNK = 128
ROPE_BASE = 10000.0
PEER_HEADS = 8
PEER_NKEYS = 128
PEER_HALF = 128
PEER_TOPK = 16
NORM_EPS = 1e-6
IN_WIDTH = 12288

VMEM_LIMIT_BYTES = 56 * 1024 * 1024

HG_CHUNK = 64
HG_SUB = 16
NOT_SELECTED_RANK = 99.0
BF16_ROWS = 16
LANES = 128
SUBLANES = 8
LOG2E = 1.4426950408889634


def _sigmoid(x):
    return 1.0 / (1.0 + jnp.exp(-x))


def _dot(a, b):
    return jnp.dot(a, b, preferred_element_type=F32)


def _dot_nt(a, b):
    return lax.dot_general(a, b, (((1,), (1,)), ((), ())), preferred_element_type=F32)


def _dot_tn(a, b):
    return lax.dot_general(a, b, (((0,), (0,)), ((), ())), preferred_element_type=F32)


def _params(sem):
    return pltpu.CompilerParams(dimension_semantics=sem, vmem_limit_bytes=VMEM_LIMIT_BYTES)


def _inproj_body(x_ref, nw_ref, w_ref, o_ref, h_scr):
    @pl.when(pl.program_id(1) == 0)
    def _():
        x = x_ref[...]
        ms = jnp.mean(x * x, axis=-1, keepdims=True)
        h_scr[...] = (x * lax.rsqrt(ms + NORM_EPS) * nw_ref[...]).astype(BF16)

    o_ref[...] = _dot(h_scr[...], w_ref[...]).astype(o_ref.dtype)


def _inproj(x2, norm_w, w_bf, out_dtype, tm=1024, tn=3072):
    t, d = x2.shape
    n = w_bf.shape[1]
    return pl.pallas_call(
        _inproj_body,
        grid=(t // tm, n // tn),
        in_specs=[
            pl.BlockSpec((tm, d), lambda i, j: (i, 0)),
            pl.BlockSpec((1, d), lambda i, j: (0, 0)),
            pl.BlockSpec((d, tn), lambda i, j: (0, j)),
        ],
        out_specs=pl.BlockSpec((tm, tn), lambda i, j: (i, j)),
        out_shape=jax.ShapeDtypeStruct((t, n), out_dtype),
        scratch_shapes=[pltpu.VMEM((tm, d), BF16)],
        compiler_params=_params(("parallel", "arbitrary")),
        name="inproj",
    )(x2, norm_w, w_bf)


def _rows4(vals, n):
    return jnp.concatenate([jnp.broadcast_to(v, (HG_SUB, n)) for v in vals], axis=0)


def _hgrn2_setup(q_ref, f_ref, i_ref, g_ref, lbp_ref, nw_ref, o_ref, st_ref, *, heads):
    c, sub, dk = HG_CHUNK, HG_SUB, HG_DK

    def init():
        st_ref[...] = jnp.zeros_like(st_ref)

    lbp = lbp_ref[...]
    mx = jnp.max(lbp, axis=0, keepdims=True)
    el = jnp.exp(lbp - mx)
    lb_all = el[0:1, :] / (el[0:1, :] + el[1:2, :])
    nw_all = nw_ref[...]

    rin = lax.broadcasted_iota(jnp.int32, (c, dk), 0) & (sub - 1)
    r8 = lax.broadcasted_iota(jnp.int32, (SUBLANES, dk), 0)
    lane_c = lax.broadcasted_iota(jnp.int32, (SUBLANES, c), 1)
    sub_shift = sub.bit_length() - 1
    rb = lax.broadcasted_iota(jnp.int32, (c, c), 0) >> sub_shift
    cb = lax.broadcasted_iota(jnp.int32, (c, c), 1) >> sub_shift
    m16 = ((rb & 1) == 1) & (cb == rb - 1)
    m32 = (rb >= 2) & (cb < 2)
    one = jnp.ones((1, dk), F32)

    def head_chunk(hh, sl):
        hc = slice(hh * dk, (hh + 1) * dk)
        lb, nw = lb_all[:, hc], nw_all[:, hc]
        q = q_ref[sl, hc].astype(F32) * (HG_DK ** -0.5)
        fl = f_ref[sl, hc].astype(F32)
        v = i_ref[sl, hc].astype(F32)
        f = lb + (1.0 - lb) * _sigmoid(fl)
        k = 1.0 - f
        lf = jnp.log(f)

        bl = lf
        for sh in (1, 2, 4, 8):
            bl = bl + jnp.where(rin >= sh, pltpu.roll(bl, sh, 0), 0.0)
        g = [bl[sub * j + sub - 1:sub * j + sub, :] for j in range(4)]
        eg = [jnp.exp(x) for x in g]

        q16 = q * jnp.exp(bl)
        k16 = k * jnp.exp(_rows4(g, dk) - bl)
        q32 = q16 * _rows4([one, one, one, eg[2]], dk)
        k32 = k16 * _rows4([eg[1], one, one, one], dk)
        qin = q16 * _rows4([one, eg[0], eg[0] * eg[1], eg[0] * eg[1] * eg[2]], dk)
        kst = k16 * _rows4([eg[1] * eg[2] * eg[3], eg[2] * eg[3], eg[3], one], dk)

        s16 = _dot_nt(q16.astype(BF16), k16.astype(BF16))
        s32 = _dot_nt(q32.astype(BF16), k32.astype(BF16))
        a = jnp.where(m16, s16, 0.0) + jnp.where(m32, s32, 0.0)

        bl2 = bl * LOG2E
        tiles = []
        for j in range(c // sub):
            lo = sub * j
            for hf in range(sub // SUBLANES):
                r0 = lo + SUBLANES * hf
                b_t, q_t = bl2[r0:r0 + SUBLANES], q[r0:r0 + SUBLANES]
                tile = a[r0:r0 + SUBLANES, :]
                for s in range(SUBLANES * (hf + 1)):
                    d = b_t - bl2[lo + s:lo + s + 1, :]
                    if s > SUBLANES * hf:
                        d = jnp.where(r8 >= s - SUBLANES * hf, d, -jnp.inf)
                    p = q_t * k[lo + s:lo + s + 1, :] * jnp.exp2(d)
                    tile = jnp.where(lane_c == lo + s, jnp.sum(p, axis=-1, keepdims=True), tile)
                tiles.append(tile)
        a = jnp.concatenate(tiles, axis=0)

        vb = v.astype(BF16)
        st = st_ref[hh]
        o = _dot(a.astype(BF16), vb) + _dot_nt(qin.astype(BF16), st.astype(BF16))

        st_ref[hh] = st * (eg[0] * eg[1] * eg[2] * eg[3]) + _dot_tn(vb, kst.astype(BF16))

        ms = jnp.mean(o * o, axis=-1, keepdims=True)
        gt = g_ref[sl, hc].astype(F32)
        o_ref[sl, hc] = (o * lax.rsqrt(ms + NORM_EPS) * nw * (gt * _sigmoid(gt))).astype(o_ref.dtype)

    def chunk(sl):
        for hh in range(heads):
            head_chunk(hh, sl)

    return init, chunk


def _ret_setup(q_ref, k_ref, v_ref, g_ref, cos_ref, sin_ref, lg_ref, o_ref, st_ref, dm_ref, qd_ref, kd_ref):
    c, half, nh = RET_CHUNK, RET_DK // 2, RET_HEADS

    def init():
        st_ref[...] = jnp.zeros_like(st_ref)
        ri = lax.broadcasted_iota(jnp.int32, (c, half), 0).astype(F32)
        ci = lax.broadcasted_iota(jnp.int32, (c, half), 1).astype(F32)
        rel = ri - ci
        for h in range(nh):
            lg = lg_ref[h][:, :half]
            dm_ref[h] = jnp.where(rel >= 0, jnp.exp(lg * jnp.maximum(rel, 0.0)), 0.0)
            qd_ref[h] = jnp.exp(lg * (ri + 1.0))
            kd_ref[h] = jnp.exp(lg * (c - 1.0 - ri))

    def cat(x1, x2):
        return jnp.concatenate([x1, x2], axis=1).astype(BF16)

    def head_chunk(h, sl, cs, sn):
        vc = slice(h * RET_DV, (h + 1) * RET_DV)

        def rot(x_ref):
            x1 = x_ref[sl, h * RET_DK:h * RET_DK + half].astype(F32)
            x2 = x_ref[sl, h * RET_DK + half:(h + 1) * RET_DK].astype(F32)
            return x1 * cs - x2 * sn, x1 * sn + x2 * cs

        q1, q2 = rot(q_ref)
        k1, k2 = rot(k_ref)
        k1 = k1 * (RET_DK ** -0.5)
        k2 = k2 * (RET_DK ** -0.5)
        vb = v_ref[sl, vc].astype(BF16)
        st = st_ref[h]
        qdec, kdec = qd_ref[h], kd_ref[h]
        cdec = jnp.exp(lg_ref[h] * float(c))

        a = _dot_nt(cat(q1, q2), cat(k1, k2)) * dm_ref[h]
        o = _dot(a.astype(BF16), vb) + _dot(cat(q1 * qdec, q2 * qdec), st.astype(BF16))
        st_ref[h] = cdec * st + _dot_tn(cat(k1 * kdec, k2 * kdec), vb)

        ms = jnp.mean(o * o, axis=-1, keepdims=True)
        gt = g_ref[sl, vc].astype(F32)
        o_ref[sl, vc] = (o * lax.rsqrt(ms + NORM_EPS) * (gt * _sigmoid(gt))).astype(o_ref.dtype)

    def chunk(sl):
        cs = cos_ref[sl, :]
        sn = sin_ref[sl, :]
        for h in range(nh):
            head_chunk(h, sl, cs, sn)

    return init, chunk


def _mix_body(hq_ref, hf_ref, hi_ref, hg_ref, lbp_ref, hnw_ref, rq_ref, rk_ref, rv_ref, rg_ref, cos_ref, sin_ref,
              lg_ref, ohg_ref, oret_ref, hst_ref, rst_ref, dm_ref, qd_ref, kd_ref, *, n_units):
    h_init, h_chunk = _hgrn2_setup(hq_ref, hf_ref, hi_ref, hg_ref, lbp_ref, hnw_ref, ohg_ref, hst_ref,
                                   heads=HG_HEADS)
    r_init, r_chunk = _ret_setup(rq_ref, rk_ref, rv_ref, rg_ref, cos_ref, sin_ref, lg_ref, oret_ref, rst_ref,
                                 dm_ref, qd_ref, kd_ref)

    @pl.when(pl.program_id(1) == 0)
    def _():
        h_init()
        r_init()

    def unit(ui, carry):
        base = ui * RET_CHUNK
        for sub in range(RET_CHUNK // HG_CHUNK):
            h_chunk(pl.ds(pl.multiple_of(base + sub * HG_CHUNK, HG_CHUNK), HG_CHUNK))
        r_chunk(pl.ds(pl.multiple_of(base, RET_CHUNK), RET_CHUNK))
        return carry

    lax.fori_loop(0, n_units, unit, 0)


def _mixers(proj3, lb_logits, hg_norm_w, cos_t, sin_t, log_gamma, ts=512):
    b, s, _ = proj3.shape
    wh = HG_HEADS * HG_DK
    wq, wv = RET_HEADS * RET_DK, RET_HEADS * RET_DV
    q_off = 4 * wh
    qb, kb = q_off // wq, q_off // wq + 1
    vb, gb = (q_off + 2 * wq) // wv, (q_off + 2 * wq) // wv + 1
    half = RET_DK // 2

    def col(w, blk):
        return pl.BlockSpec((None, ts, w), lambda bi, si, blk=blk: (bi, si, blk))

    return pl.pallas_call(
        functools.partial(_mix_body, n_units=ts // RET_CHUNK),
        grid=(b, s // ts),
        in_specs=[
            col(wh, 0), col(wh, 1), col(wh, 2), col(wh, 3),
            pl.BlockSpec((2, wh), lambda bi, si: (0, 0)),
            pl.BlockSpec((1, wh), lambda bi, si: (0, 0)),
            col(wq, qb), col(wq, kb), col(wv, vb), col(wv, gb),
            pl.BlockSpec((ts, half), lambda bi, si: (si, 0)),
            pl.BlockSpec((ts, half), lambda bi, si: (si, 0)),
            pl.BlockSpec((RET_HEADS, 1, RET_DV), lambda bi, si: (0, 0, 0)),
        ],
        out_specs=[
            pl.BlockSpec((None, ts, wh), lambda bi, si: (bi, si, 0)),
            pl.BlockSpec((None, ts, wv), lambda bi, si: (bi, si, 0)),
        ],
        out_shape=[
            jax.ShapeDtypeStruct((b, s, wh), BF16),
            jax.ShapeDtypeStruct((b, s, wv), BF16),
        ],
        scratch_shapes=[
            pltpu.VMEM((HG_HEADS, HG_DV, HG_DK), F32),
            pltpu.VMEM((RET_HEADS, RET_DK, RET_DV), F32),
            pltpu.VMEM((RET_HEADS, RET_CHUNK, RET_CHUNK), F32),
            pltpu.VMEM((RET_HEADS, RET_CHUNK, half), F32),
            pltpu.VMEM((RET_HEADS, RET_CHUNK, half), F32),
        ],
        compiler_params=_params(("parallel", "arbitrary")),
        name="mixers",
    )(proj3, proj3, proj3, proj3, lb_logits, hg_norm_w, proj3, proj3, proj3, proj3, cos_t, sin_t, log_gamma)


def _merge_body(ohg_ref, oret_ref, ga_ref, gb_ref, x_ref, wbh_ref, wbr_ref, wo_ref, nfw_ref, wq_ref, keys_ref,
                x1_ref, h2t_ref, sc_ref):
    yh = _dot(ohg_ref[...], wbh_ref[...])
    yr = _dot(oret_ref[...], wbr_ref[...])
    m = _sigmoid(ga_ref[...].astype(F32)) * yh + _sigmoid(gb_ref[...].astype(F32)) * yr
    x1 = x_ref[...] + _dot(m.astype(BF16), wo_ref[...])
    x1_ref[...] = x1
    ms = jnp.mean(x1 * x1, axis=-1, keepdims=True)
    h2 = x1 * lax.rsqrt(ms + NORM_EPS) * nfw_ref[...]
    h2t_ref[...] = h2.T.astype(BF16)
    q = _dot(h2.astype(BF16), wq_ref[...]).astype(BF16)
    for h in range(PEER_HEADS):
        for p in range(2):
            c0 = (2 * h + p) * PEER_HALF
            sc_ref[h, p] = _dot_nt(keys_ref[h, p], q[:, c0:c0 + PEER_HALF])


def _merge(o_hg, o_ret, proj, x2, wbh, wbr, wo, nfw, wq, keys_bf, tm=512):
    t, d = x2.shape
    ga_blk = (IN_WIDTH - 2 * D_MODEL) // D_MODEL
    nh, nk = PEER_HEADS, PEER_NKEYS

    def full(a):
        return pl.BlockSpec(a.shape, lambda i: (0, 0), pipeline_mode=pl.Buffered(1))

    return pl.pallas_call(
        _merge_body,
        grid=(t // tm,),
        in_specs=[
            pl.BlockSpec((tm, o_hg.shape[1]), lambda i: (i, 0)),
            pl.BlockSpec((tm, o_ret.shape[1]), lambda i: (i, 0)),
            pl.BlockSpec((tm, d), lambda i: (i, ga_blk)),
            pl.BlockSpec((tm, d), lambda i: (i, ga_blk + 1)),
            pl.BlockSpec((tm, d), lambda i: (i, 0)),
            full(wbh), full(wbr), full(wo), full(nfw), full(wq),
            pl.BlockSpec(keys_bf.shape, lambda i: (0, 0, 0, 0), pipeline_mode=pl.Buffered(1)),
        ],
        out_specs=[
            pl.BlockSpec((tm, d), lambda i: (i, 0)),
            pl.BlockSpec((d, tm), lambda i: (0, i)),
            pl.BlockSpec((nh, 2, nk, tm), lambda i: (0, 0, 0, i)),
        ],
        out_shape=[
            jax.ShapeDtypeStruct((t, d), F32),
            jax.ShapeDtypeStruct((d, t), BF16),
            jax.ShapeDtypeStruct((nh, 2, nk, t), F32),
        ],
        compiler_params=_params(("parallel",)),
        name="merge",
    )(o_hg, o_ret, proj, proj, x2, wbh, wbr, wo, nfw, wq, keys_bf)


_CAND_NB = (16, 8, 5, 4, 3, 2, 2, 2)
_CAND_ROWS = 16 + 8 * 7 + 8


def _extract_top(vals, n_rows, rounds, on_round, break_ties):
    iota = lax.broadcasted_iota(jnp.int32, vals.shape, 0).astype(F32)
    for r in range(rounds):
        m = jnp.max(vals, axis=0, keepdims=True)
        sel = vals == m
        if break_ties:
            idx = jnp.min(jnp.where(sel, iota, float(n_rows)), axis=0, keepdims=True)
            sel = iota == idx
        vals = jnp.where(sel, -jnp.inf, vals)
        on_round(r, m, sel)
    return vals


_GELU_K0 = -2.0 * 0.7978845608028654 * 1.4426950408889634
_GELU_K1 = _GELU_K0 * 0.044715


def _gelu_tanh(x):
    return x / (1.0 + jnp.exp2(x * (x * x * _GELU_K1 + _GELU_K0)))


def _ffn_body(sc_ref, h2t_ref, u_ref, vt_ref, x1_ref, fnw_ref, o_ref,
              acc_ref, p_ref, e1_scr, n1_scr, e2_scr, r2_scr, ss_scr, rk_scr, top_scr, ex_scr,
              st_e1, st_n1, st_e2, st_r2, *, i1_per_blk, n_groups, n_units):
    s = pl.program_id(0)
    nh, nk, k, lanes = PEER_HEADS, PEER_NKEYS, PEER_TOPK, LANES
    new, old = 0, 1
    cur, nxt = 0, 1
    h2 = jnp.clip(s - 1, 0, n_units - 1) % nh
    je = jnp.clip(s - (nh + 1), 0, n_units - 1) % nh
    r8 = lax.broadcasted_iota(jnp.int32, (8, lanes), 0)

    @pl.when(s == 0)
    def _():
        for r in (ss_scr, rk_scr, top_scr):
            r[...] = jnp.zeros_like(r)

    @pl.when(je == 0)
    def _():
        acc_ref[...] = jnp.zeros_like(acc_ref)

    def stage1(g, break_ties):
        excess = jnp.zeros((1, lanes), F32)
        for p in range(2):
            sc = sc_ref[p, :, g * lanes:(g + 1) * lanes]
            ss_scr[new, g, p] = sc
            rk_scr[new, g, p] = jnp.full((nk, lanes), NOT_SELECTED_RANK, F32)

            def on_round(r, m, sel, p=p):
                top_scr[new, g, p, r:r + 1, :] = m
                pltpu.store(rk_scr.at[new, g, p], jnp.full((nk, lanes), float(r), F32), mask=sel)

            left = _extract_top(sc, nk, k, on_round, break_ties)
            removed = jnp.sum(jnp.where(left == -jnp.inf, 1.0, 0.0), axis=0, keepdims=True)
            excess = excess + (removed - float(k))
        return excess

    def stage2(g, break_ties):
        ln = slice(g * lanes, (g + 1) * lanes)
        t1, t2 = top_scr[old, g, 0], top_scr[old, g, 1]
        pieces = [t2 + t1[0:1, :], t2[0:8] + t1[1:2, :]]
        for a in range(2, 8):
            pieces.append(jnp.where(r8 < _CAND_NB[a], t2[0:8] + t1[a:a + 1, :], -jnp.inf))
        pieces.append(t1[8:16] + t2[0:1, :])
        cand = jnp.concatenate(pieces, axis=0)
        best = []
        left = _extract_top(cand, _CAND_ROWS, k, lambda r, m, sel: best.append(m), break_ties)
        selm = jnp.where((left == -jnp.inf) & (cand > -jnp.inf), 1.0, 0.0)
        z = jnp.ones_like(best[0])
        for m in best[1:]:
            z = z + jnp.exp(m - best[0])
        inv_z = 1.0 / z
        n_rows = [jnp.sum(selm[0:16], axis=0, keepdims=True)]
        for a in range(1, 8):
            lo = 16 + 8 * (a - 1)
            n_rows.append(jnp.sum(selm[lo:lo + 8], axis=0, keepdims=True))
        for r in range(8):
            n_rows.append(selm[_CAND_ROWS - 8 + r:_CAND_ROWS - 7 + r, :])
        total = n_rows[0]
        for x in n_rows[1:]:
            total = total + x

        rank1 = rk_scr[old, g, 0]
        n1 = jnp.zeros((nk, lanes), F32)
        for a in range(k):
            n1 = jnp.where(rank1 == float(a), n_rows[a], n1)
        st_n1[:, ln] = n1
        st_e1[:, ln] = jnp.exp(ss_scr[old, g, 0] - t1[0:1, :]) * inv_z
        st_e2[:, ln] = jnp.exp(ss_scr[old, g, 1] - t2[0:1, :]).astype(st_e2.dtype)
        st_r2[:, ln] = rk_scr[old, g, 1].astype(st_r2.dtype)
        return total - float(k)

    def select(break_ties):
        worst1 = jnp.zeros((1, lanes), F32)
        worst2 = jnp.zeros((1, lanes), F32)
        for g in range(n_groups):
            worst1 = jnp.maximum(worst1, stage1(g, break_ties))
            worst2 = jnp.maximum(worst2, stage2(g, break_ties))
        ex_scr[...] = jnp.maximum(jnp.where(s < n_units, worst1, 0.0),
                                  jnp.where((s >= 1) & (s <= n_units), worst2, 0.0))

    def evaluate_experts():
        act = _gelu_tanh(_dot(u_ref[...], h2t_ref[...])).astype(BF16)
        tb = act.shape[1]
        zero = jnp.zeros((BF16_ROWS, tb), BF16)
        for il in range(i1_per_blk):
            row = pl.ds(je * i1_per_blk + il, 1)
            n1 = [jnp.broadcast_to(n1_scr[cur, h, row, :], (BF16_ROWS, tb)).astype(BF16) for h in range(nh)]
            e1 = [jnp.broadcast_to(e1_scr[cur, h, row, :], (BF16_ROWS, tb)).astype(BF16) for h in range(nh)]
            for m in range(nk // BF16_ROWS):
                rows = slice(m * BF16_ROWS, (m + 1) * BF16_ROWS)
                w = None
                for h in range(nh):
                    sel = r2_scr[cur, h, rows, :] < n1[h]
                    prod = e2_scr[cur, h, rows, :] * e1[h]
                    w = jnp.where(sel, prod, zero) if w is None else jnp.where(sel, w + prod, w)
                lo = il * nk + m * BF16_ROWS
                p_ref[lo:lo + BF16_ROWS, :] = w * act[lo:lo + BF16_ROWS, :]
        acc_ref[...] += _dot(vt_ref[...], p_ref[...])

    @pl.when(s <= nh)
    def _():
        select(False)

    @pl.when((s > nh) & (s <= n_units))
    def _():
        select(False)
        evaluate_experts()

    @pl.when(s > n_units)
    def _():
        ex_scr[...] = jnp.zeros_like(ex_scr)
        evaluate_experts()

    @pl.when(jnp.max(ex_scr[...]) > 0.0)
    def _():
        select(True)

    @pl.when(s <= n_units)
    def _():
        for r in (ss_scr, rk_scr, top_scr):
            r[old] = r[new]
        n1_scr[nxt, h2] = st_n1[...]
        e1_scr[nxt, h2] = st_e1[...]
        e2_scr[nxt, h2] = st_e2[...]
        r2_scr[nxt, h2] = st_r2[...]

    @pl.when((s >= nh) & (s % nh == 0))
    def _():
        for r in (n1_scr, e1_scr, e2_scr, r2_scr):
            r[cur] = r[nxt]

    @pl.when((s > nh) & (je == nh - 1))
    def _():
        y = x1_ref[...] + acc_ref[...].T
        ms = jnp.mean(y * y, axis=-1, keepdims=True)
        o_ref[...] = y * lax.rsqrt(ms + NORM_EPS) * fnw_ref[...]


def _peer_ffn(scores, h2t, u_bf, vt_bf, x1, fnw, tb=512):
    t, d = x1.shape
    ne = u_bf.shape[0]
    nh, nk = PEER_HEADS, PEER_NKEYS
    eb = ne // nh
    ng = tb // LANES
    n_units = (t // tb) * nh

    def unit1(s):
        u = jnp.minimum(s, n_units - 1)
        return u // nh, u % nh

    def eblk(s):
        e = jnp.clip(s - (nh + 1), 0, n_units - 1)
        return e // nh, e % nh

    return pl.pallas_call(
        functools.partial(_ffn_body, i1_per_blk=eb // nk, n_groups=ng, n_units=n_units),
        grid=(n_units + nh + 1,),
        in_specs=[
            pl.BlockSpec((None, 2, nk, tb), lambda s: (unit1(s)[1], 0, 0, unit1(s)[0])),
            pl.BlockSpec((d, tb), lambda s: (0, eblk(s)[0])),
            pl.BlockSpec((eb, d), lambda s: (eblk(s)[1], 0)),
            pl.BlockSpec((d, eb), lambda s: (0, eblk(s)[1])),
            pl.BlockSpec((tb, d), lambda s: (eblk(s)[0], 0)),
            pl.BlockSpec((1, d), lambda s: (0, 0)),
        ],
        out_specs=pl.BlockSpec((tb, d), lambda s: (eblk(s)[0], 0)),
        out_shape=jax.ShapeDtypeStruct((t, d), F32),
        scratch_shapes=[
            pltpu.VMEM((d, tb), F32),
            pltpu.VMEM((eb, tb), BF16),
            pltpu.VMEM((2, nh, nk, tb), F32),
            pltpu.VMEM((2, nh, nk, tb), F32),
            pltpu.VMEM((2, nh, nk, tb), BF16),
            pltpu.VMEM((2, nh, nk, tb), BF16),
            pltpu.VMEM((2, ng, 2, nk, LANES), F32),
            pltpu.VMEM((2, ng, 2, nk, LANES), F32),
            pltpu.VMEM((2, ng, 2, PEER_TOPK, LANES), F32),
            pltpu.VMEM((1, LANES), F32),
            pltpu.VMEM((nk, tb), F32), pltpu.VMEM((nk, tb), F32),
            pltpu.VMEM((nk, tb), BF16), pltpu.VMEM((nk, tb), BF16),
        ],
        compiler_params=_params(("arbitrary",)),
        name="peer_ffn",
    )(scores, h2t, u_bf, vt_bf, x1, fnw)


def _rope_tables(s):
    half = RET_DK // 2
    inv = ROPE_BASE ** (-jnp.arange(half, dtype=F32) / half)
    ang = jnp.arange(s, dtype=F32)[:, None] * inv[None, :]
    return jnp.cos(ang), jnp.sin(ang)


def kernel(x, norm_mix_w, w_in, hg_lower_bounds, hg_norm_w, w_branch_hg, w_branch_ret, w_out, norm_ffn_w,
           peer_w_q, peer_sub_keys, expert_u, expert_v, final_norm_w):
    b, s, d = x.shape
    t = b * s
    x2 = x.reshape(t, d)

    proj = _inproj(x2, norm_mix_w[0:1], w_in[0].astype(BF16), BF16)
    proj3 = proj.reshape(b, s, IN_WIDTH)

    cos_t, sin_t = _rope_tables(s)
    log_gamma = jnp.log(1.0 - jnp.exp2(-5.0 - jnp.arange(RET_HEADS, dtype=F32)))
    log_gamma = jnp.broadcast_to(log_gamma[:, None, None], (RET_HEADS, 1, RET_DV))
    o_hg, o_ret = _mixers(proj3, hg_lower_bounds, hg_norm_w[0:1], cos_t, sin_t, log_gamma)

    x1, h2t, scores = _merge(
        o_hg.reshape(t, -1), o_ret.reshape(t, -1), proj, x2,
        w_branch_hg[0].astype(BF16), w_branch_ret[0].astype(BF16), w_out[0].astype(BF16),
        norm_ffn_w[0:1], peer_w_q[0].astype(BF16), peer_sub_keys[0].astype(BF16))

    out = _peer_ffn(scores, h2t, expert_u[0].astype(BF16), expert_v[0].T.astype(BF16), x1, final_norm_w[None, :])
    return out.reshape(b, s, d)
```

```python
import functools

import jax
import jax.numpy as jnp
from jax import lax
from jax.experimental import pallas as pl
from jax.experimental.pallas import tpu as pltpu

F32 = jnp.float32
BF16 = jnp.bfloat16

D_MODEL = 1024
HG_HEADS = 8
HG_DK = 128
HG_DV = 128
RET_HEADS = 4
RET_DK = 256
RET_DV = 512
RET_CHUNK = 128
ROPE_BASE = 10000.0
PEER_HEADS = 8
PEER_NKEYS = 128
PEER_HALF = 128
PEER_TOPK = 16
NORM_EPS = 1e-6
IN_WIDTH = 12288

VMEM_LIMIT_BYTES = 56 * 1024 * 1024

HG_CHUNK = 64
HG_SUB = 16
NOT_SELECTED_RANK = 99.0
BF16_ROWS = 16
LANES = 128
SUBLANES = 8
LOG2E = 1.4426950408889634


def _sigmoid(x):
    return 1.0 / (1.0 + jnp.exp(-x))


def _dot(a, b):
    return jnp.dot(a, b, preferred_element_type=F32)


def _dot_nt(a, b):
    return lax.dot_general(a, b, (((1,), (1,)), ((), ())), preferred_element_type=F32)


def _dot_tn(a, b):
    return lax.dot_general(a, b, (((0,), (0,)), ((), ())), preferred_element_type=F32)


def _params(sem):
    return pltpu.CompilerParams(dimension_semantics=sem, vmem_limit_bytes=VMEM_LIMIT_BYTES)


def _inproj_body(x_ref, nw_ref, w_ref, o_ref):
    x = x_ref[...]
    ms = jnp.mean(x * x, axis=-1, keepdims=True)
    h = (x * lax.rsqrt(ms + NORM_EPS) * nw_ref[...]).astype(BF16)
    o_ref[...] = _dot(h, w_ref[...]).astype(o_ref.dtype)


def _inproj(x2, norm_w, w_bf, out_dtype, tm=1024, tn=3072):
    t, d = x2.shape
    n = w_bf.shape[1]
    return pl.pallas_call(
        _inproj_body,
        grid=(n // tn, t // tm),
        in_specs=[
            pl.BlockSpec((tm, d), lambda j, i: (i, 0)),
            pl.BlockSpec((1, d), lambda j, i: (0, 0)),
            pl.BlockSpec((d, tn), lambda j, i: (0, j)),
        ],
        out_specs=pl.BlockSpec((tm, tn), lambda j, i: (i, j)),
        out_shape=jax.ShapeDtypeStruct((t, n), out_dtype),
        compiler_params=_params(("arbitrary", "arbitrary")),
        name="inproj",
    )(x2, norm_w, w_bf)


def _rows4(vals, n):
    return jnp.concatenate([jnp.broadcast_to(v, (HG_SUB, n)) for v in vals], axis=0)


def _hgrn2_setup(q_ref, f_ref, i_ref, g_ref, lbp_ref, nw_ref, o_ref, st_ref, *, heads):
    c, sub, dk = HG_CHUNK, HG_SUB, HG_DK

    def init():
        st_ref[...] = jnp.zeros_like(st_ref)

    lbp = lbp_ref[...]
    mx = jnp.max(lbp, axis=0, keepdims=True)
    el = jnp.exp(lbp - mx)
    lb_all = el[0:1, :] / (el[0:1, :] + el[1:2, :])
    nw_all = nw_ref[...]

    rin = lax.broadcasted_iota(jnp.int32, (c, dk), 0) & (sub - 1)
    r8 = lax.broadcasted_iota(jnp.int32, (SUBLANES, dk), 0)
    lane_c = lax.broadcasted_iota(jnp.int32, (SUBLANES, c), 1)
    sub_shift = sub.bit_length() - 1
    rb = lax.broadcasted_iota(jnp.int32, (c, c), 0) >> sub_shift
    cb = lax.broadcasted_iota(jnp.int32, (c, c), 1) >> sub_shift
    m16 = ((rb & 1) == 1) & (cb == rb - 1)
    m32 = (rb >= 2) & (cb < 2)
    one = jnp.ones((1, dk), F32)

    def head_chunk(hh, sl):
        hc = slice(hh * dk, (hh + 1) * dk)
        lb, nw = lb_all[:, hc], nw_all[:, hc]
        q = q_ref[sl, hc].astype(F32) * (HG_DK ** -0.5)
        fl = f_ref[sl, hc].astype(F32)
        v = i_ref[sl, hc].astype(F32)
        f = lb + (1.0 - lb) * _sigmoid(fl)
        k = 1.0 - f
        lf = jnp.log(f)

        bl = lf
        for sh in (1, 2, 4, 8):
            bl = bl + jnp.where(rin >= sh, pltpu.roll(bl, sh, 0), 0.0)
        g = [bl[sub * j + sub - 1:sub * j + sub, :] for j in range(4)]
        eg = [jnp.exp(x) for x in g]

        q16 = q * jnp.exp(bl)
        k16 = k * jnp.exp(_rows4(g, dk) - bl)
        q32 = q16 * _rows4([one, one, one, eg[2]], dk)
        k32 = k16 * _rows4([eg[1], one, one, one], dk)
        qin = q16 * _rows4([one, eg[0], eg[0] * eg[1], eg[0] * eg[1] * eg[2]], dk)
        kst = k16 * _rows4([eg[1] * eg[2] * eg[3], eg[2] * eg[3], eg[3], one], dk)

        s16 = _dot_nt(q16.astype(BF16), k16.astype(BF16))
        s32 = _dot_nt(q32.astype(BF16), k32.astype(BF16))
        a = jnp.where(m16, s16, 0.0) + jnp.where(m32, s32, 0.0)

        bl2 = bl * LOG2E
        tiles = []
        for j in range(c // sub):
            lo = sub * j
            for hf in range(sub // SUBLANES):
                r0 = lo + SUBLANES * hf
                b_t, q_t = bl2[r0:r0 + SUBLANES], q[r0:r0 + SUBLANES]
                tile = a[r0:r0 + SUBLANES, :]
                for s in range(SUBLANES * (hf + 1)):
                    d = b_t - bl2[lo + s:lo + s + 1, :]
                    if s > SUBLANES * hf:
                        d = jnp.where(r8 >= s - SUBLANES * hf, d, -jnp.inf)
                    p = q_t * k[lo + s:lo + s + 1, :] * jnp.exp2(d)
                    tile = jnp.where(lane_c == lo + s, jnp.sum(p, axis=-1, keepdims=True), tile)
                tiles.append(tile)
        a = jnp.concatenate(tiles, axis=0)

        vb = v.astype(BF16)
        st = st_ref[hh]
        o = _dot(a.astype(BF16), vb) + _dot_nt(qin.astype(BF16), st.astype(BF16))

        st_ref[hh] = st * (eg[0] * eg[1] * eg[2] * eg[3]) + _dot_tn(vb, kst.astype(BF16))

        ms = jnp.mean(o * o, axis=-1, keepdims=True)
        gt = g_ref[sl, hc].astype(F32)
        o_ref[sl, hc] = (o * lax.rsqrt(ms + NORM_EPS) * nw * (gt * _sigmoid(gt))).astype(o_ref.dtype)

    def chunk(sl):
        for hh in range(heads):
            head_chunk(hh, sl)

    return init, chunk


def _ret_setup(q_ref, k_ref, v_ref, g_ref, cos_ref, sin_ref, lg_ref, o_ref, st_ref, dm_ref, qd_ref, kd_ref):
    c, half, nh = RET_CHUNK, RET_DK // 2, RET_HEADS

    def init():
        st_ref[...] = jnp.zeros_like(st_ref)
        ri = lax.broadcasted_iota(jnp.int32, (c, half), 0).astype(F32)
        ci = lax.broadcasted_iota(jnp.int32, (c, half), 1).astype(F32)
        rel = ri - ci
        for h in range(nh):
            lg = lg_ref[h][:, :half]
            dm_ref[h] = jnp.where(rel >= 0, jnp.exp(lg * jnp.maximum(rel, 0.0)), 0.0)
            qd_ref[h] = jnp.exp(lg * (ri + 1.0))
            kd_ref[h] = jnp.exp(lg * (c - 1.0 - ri))

    def cat(x1, x2):
        return jnp.concatenate([x1, x2], axis=1).astype(BF16)

    def head_chunk(h, sl, cs, sn):
        vc = slice(h * RET_DV, (h + 1) * RET_DV)

        def rot(x_ref):
            x1 = x_ref[sl, h * RET_DK:h * RET_DK + half].astype(F32)
            x2 = x_ref[sl, h * RET_DK + half:(h + 1) * RET_DK].astype(F32)
            return x1 * cs - x2 * sn, x1 * sn + x2 * cs

        q1, q2 = rot(q_ref)
        k1, k2 = rot(k_ref)
        k1 = k1 * (RET_DK ** -0.5)
        k2 = k2 * (RET_DK ** -0.5)
        vb = v_ref[sl, vc].astype(BF16)
        st = st_ref[h]
        qdec, kdec = qd_ref[h], kd_ref[h]
        cdec = jnp.exp(lg_ref[h] * float(c))

        a = _dot_nt(cat(q1, q2), cat(k1, k2)) * dm_ref[h]
        o = _dot(a.astype(BF16), vb) + _dot(cat(q1 * qdec, q2 * qdec), st.astype(BF16))
        st_ref[h] = cdec * st + _dot_tn(cat(k1 * kdec, k2 * kdec), vb)

        ms = jnp.mean(o * o, axis=-1, keepdims=True)
        gt = g_ref[sl, vc].astype(F32)
        o_ref[sl, vc] = (o * lax.rsqrt(ms + NORM_EPS) * (gt * _sigmoid(gt))).astype(o_ref.dtype)

    def chunk(sl):
        cs = cos_ref[sl, :]
        sn = sin_ref[sl, :]
        for h in range(nh):
            head_chunk(h, sl, cs, sn)

    return init, chunk


def _mix_body(hq_ref, hf_ref, hi_ref, hg_ref, lbp_ref, hnw_ref, rq_ref, rk_ref, rv_ref, rg_ref, cos_ref, sin_ref,
              lg_ref, ohg_ref, oret_ref, hst_ref, rst_ref, dm_ref, qd_ref, kd_ref, *, n_units):
    h_init, h_chunk = _hgrn2_setup(hq_ref, hf_ref, hi_ref, hg_ref, lbp_ref, hnw_ref, ohg_ref, hst_ref,
                                   heads=HG_HEADS)
    r_init, r_chunk = _ret_setup(rq_ref, rk_ref, rv_ref, rg_ref, cos_ref, sin_ref, lg_ref, oret_ref, rst_ref,
                                 dm_ref, qd_ref, kd_ref)

    @pl.when(pl.program_id(1) == 0)
    def _():
        h_init()
        r_init()

    def unit(ui, carry):
        base = ui * RET_CHUNK
        for sub in range(RET_CHUNK // HG_CHUNK):
            h_chunk(pl.ds(pl.multiple_of(base + sub * HG_CHUNK, HG_CHUNK), HG_CHUNK))
        r_chunk(pl.ds(pl.multiple_of(base, RET_CHUNK), RET_CHUNK))
        return carry

    lax.fori_loop(0, n_units, unit, 0)


def _mixers(proj3, lb_logits, hg_norm_w, cos_t, sin_t, log_gamma, ts=512):
    b, s, _ = proj3.shape
    wh = HG_HEADS * HG_DK
    wq, wv = RET_HEADS * RET_DK, RET_HEADS * RET_DV
    q_off = 4 * wh
    qb, kb = q_off // wq, q_off // wq + 1
    vb, gb = (q_off + 2 * wq) // wv, (q_off + 2 * wq) // wv + 1
    half = RET_DK // 2

    def col(w, blk):
        return pl.BlockSpec((None, ts, w), lambda bi, si, blk=blk: (bi, si, blk))

    return pl.pallas_call(
        functools.partial(_mix_body, n_units=ts // RET_CHUNK),
        grid=(b, s // ts),
        in_specs=[
            col(wh, 0), col(wh, 1), col(wh, 2), col(wh, 3),
            pl.BlockSpec((2, wh), lambda bi, si: (0, 0)),
            pl.BlockSpec((1, wh), lambda bi, si: (0, 0)),
            col(wq, qb), col(wq, kb), col(wv, vb), col(wv, gb),
            pl.BlockSpec((ts, half), lambda bi, si: (si, 0)),
            pl.BlockSpec((ts, half), lambda bi, si: (si, 0)),
            pl.BlockSpec((RET_HEADS, 1, RET_DV), lambda bi, si: (0, 0, 0)),
        ],
        out_specs=[
            pl.BlockSpec((None, ts, wh), lambda bi, si: (bi, si, 0)),
            pl.BlockSpec((None, ts, wv), lambda bi, si: (bi, si, 0)),
        ],
        out_shape=[
            jax.ShapeDtypeStruct((b, s, wh), BF16),
            jax.ShapeDtypeStruct((b, s, wv), BF16),
        ],
        scratch_shapes=[
            pltpu.VMEM((HG_HEADS, HG_DV, HG_DK), F32),
            pltpu.VMEM((RET_HEADS, RET_DK, RET_DV), F32),
            pltpu.VMEM((RET_HEADS, RET_CHUNK, RET_CHUNK), F32),
            pltpu.VMEM((RET_HEADS, RET_CHUNK, half), F32),
            pltpu.VMEM((RET_HEADS, RET_CHUNK, half), F32),
        ],
        compiler_params=_params(("parallel", "arbitrary")),
        name="mixers",
    )(proj3, proj3, proj3, proj3, lb_logits, hg_norm_w, proj3, proj3, proj3, proj3, cos_t, sin_t, log_gamma)


def _merge_body(ohg_ref, oret_ref, ga_ref, gb_ref, x_ref, wbh_ref, wbr_ref, wo_ref, nfw_ref, wq_ref, keys_ref,
                x1_ref, h2t_ref, sc_ref):
    yh = _dot(ohg_ref[...], wbh_ref[...])
    yr = _dot(oret_ref[...], wbr_ref[...])
    m = _sigmoid(ga_ref[...].astype(F32)) * yh + _sigmoid(gb_ref[...].astype(F32)) * yr
    x1 = x_ref[...] + _dot(m.astype(BF16), wo_ref[...])
    x1_ref[...] = x1
    ms = jnp.mean(x1 * x1, axis=-1, keepdims=True)
    h2 = x1 * lax.rsqrt(ms + NORM_EPS) * nfw_ref[...]
    h2t_ref[...] = h2.T.astype(BF16)
    q = _dot(h2.astype(BF16), wq_ref[...]).astype(BF16)
    for h in range(PEER_HEADS):
        for p in range(2):
            c0 = (2 * h + p) * PEER_HALF
            sc_ref[h, p] = _dot_nt(keys_ref[h, p], q[:, c0:c0 + PEER_HALF])


def _merge(o_hg, o_ret, proj, x2, wbh, wbr, wo, nfw, wq, keys_bf, tm=512):
    t, d = x2.shape
    ga_blk = (IN_WIDTH - 2 * D_MODEL) // D_MODEL
    nh, nk = PEER_HEADS, PEER_NKEYS

    def full(a):
        return pl.BlockSpec(a.shape, lambda i: (0, 0), pipeline_mode=pl.Buffered(1))

    return pl.pallas_call(
        _merge_body,
        grid=(t // tm,),
        in_specs=[
            pl.BlockSpec((tm, o_hg.shape[1]), lambda i: (i, 0)),
            pl.BlockSpec((tm, o_ret.shape[1]), lambda i: (i, 0)),
            pl.BlockSpec((tm, d), lambda i: (i, ga_blk)),
            pl.BlockSpec((tm, d), lambda i: (i, ga_blk + 1)),
            pl.BlockSpec((tm, d), lambda i: (i, 0)),
            full(wbh), full(wbr), full(wo), full(nfw), full(wq),
            pl.BlockSpec(keys_bf.shape, lambda i: (0, 0, 0, 0), pipeline_mode=pl.Buffered(1)),
        ],
        out_specs=[
            pl.BlockSpec((tm, d), lambda i: (i, 0)),
            pl.BlockSpec((d, tm), lambda i: (0, i)),
            pl.BlockSpec((nh, 2, nk, tm), lambda i: (0, 0, 0, i)),
        ],
        out_shape=[
            jax.ShapeDtypeStruct((t, d), F32),
            jax.ShapeDtypeStruct((d, t), BF16),
            jax.ShapeDtypeStruct((nh, 2, nk, t), F32),
        ],
        compiler_params=_params(("parallel",)),
        name="merge",
    )(o_hg, o_ret, proj, proj, x2, wbh, wbr, wo, nfw, wq, keys_bf)


_CAND_NB = (16, 8, 5, 4, 3, 2, 2, 2)
_CAND_ROWS = 16 + 8 * 7 + 8


def _extract_top(vals, n_rows, rounds, on_round, break_ties):
    iota = lax.broadcasted_iota(jnp.int32, vals.shape, 0).astype(F32)
    for r in range(rounds):
        m = jnp.max(vals, axis=0, keepdims=True)
        sel = vals == m
        if break_ties:
            idx = jnp.min(jnp.where(sel, iota, float(n_rows)), axis=0, keepdims=True)
            sel = iota == idx
        vals = jnp.where(sel, -jnp.inf, vals)
        on_round(r, m, sel)
    return vals


_GELU_K0 = -2.0 * 0.7978845608028654 * 1.4426950408889634
_GELU_K1 = _GELU_K0 * 0.044715


def _gelu_tanh(x):
    return x / (1.0 + jnp.exp2(x * (x * x * _GELU_K1 + _GELU_K0)))


def _ffn_body(sc_ref, h2t_ref, u_ref, vt_ref, x1_ref, fnw_ref, o_ref,
              acc_ref, p_ref, e1_scr, n1_scr, e2_scr, r2_scr, ss_scr, rk_scr, top_scr, ex_scr,
              st_e1, st_n1, st_e2, st_r2, *, i1_per_blk, n_groups, n_units):
    s = pl.program_id(0)
    nh, nk, k, lanes = PEER_HEADS, PEER_NKEYS, PEER_TOPK, LANES
    new, old = 0, 1
    cur, nxt = 0, 1
    h2 = jnp.clip(s - 1, 0, n_units - 1) % nh
    je = jnp.clip(s - (nh + 1), 0, n_units - 1) % nh
    r8 = lax.broadcasted_iota(jnp.int32, (8, lanes), 0)

    @pl.when(s == 0)
    def _():
        for r in (ss_scr, rk_scr, top_scr):
            r[...] = jnp.zeros_like(r)

    @pl.when(je == 0)
    def _():
        acc_ref[...] = jnp.zeros_like(acc_ref)

    def stage1(g, break_ties):
        excess = jnp.zeros((1, lanes), F32)
        for p in range(2):
            sc = sc_ref[p, :, g * lanes:(g + 1) * lanes]
            ss_scr[new, g, p] = sc
            rk_scr[new, g, p] = jnp.full((nk, lanes), NOT_SELECTED_RANK, F32)

            def on_round(r, m, sel, p=p):
                top_scr[new, g, p, r:r + 1, :] = m
                pltpu.store(rk_scr.at[new, g, p], jnp.full((nk, lanes), float(r), F32), mask=sel)

            left = _extract_top(sc, nk, k, on_round, break_ties)
            removed = jnp.sum(jnp.where(left == -jnp.inf, 1.0, 0.0), axis=0, keepdims=True)
            excess = excess + (removed - float(k))
        return excess

    def stage2(g, break_ties):
        ln = slice(g * lanes, (g + 1) * lanes)
        t1, t2 = top_scr[old, g, 0], top_scr[old, g, 1]
        pieces = [t2 + t1[0:1, :], t2[0:8] + t1[1:2, :]]
        for a in range(2, 8):
            pieces.append(jnp.where(r8 < _CAND_NB[a], t2[0:8] + t1[a:a + 1, :], -jnp.inf))
        pieces.append(t1[8:16] + t2[0:1, :])
        cand = jnp.concatenate(pieces, axis=0)
        best = []
        left = _extract_top(cand, _CAND_ROWS, k, lambda r, m, sel: best.append(m), break_ties)
        selm = jnp.where((left == -jnp.inf) & (cand > -jnp.inf), 1.0, 0.0)
        z = jnp.ones_like(best[0])
        for m in best[1:]:
            z = z + jnp.exp(m - best[0])
        inv_z = 1.0 / z
        n_rows = [jnp.sum(selm[0:16], axis=0, keepdims=True)]
        for a in range(1, 8):
            lo = 16 + 8 * (a - 1)
            n_rows.append(jnp.sum(selm[lo:lo + 8], axis=0, keepdims=True))
        for r in range(8):
            n_rows.append(selm[_CAND_ROWS - 8 + r:_CAND_ROWS - 7 + r, :])
        total = n_rows[0]
        for x in n_rows[1:]:
            total = total + x

        rank1 = rk_scr[old, g, 0]
        n1 = jnp.zeros((nk, lanes), F32)
        for a in range(k):
            n1 = jnp.where(rank1 == float(a), n_rows[a], n1)
        st_n1[:, ln] = n1
        st_e1[:, ln] = jnp.exp(ss_scr[old, g, 0] - t1[0:1, :]) * inv_z
        st_e2[:, ln] = jnp.exp(ss_scr[old, g, 1] - t2[0:1, :]).astype(st_e2.dtype)
        st_r2[:, ln] = rk_scr[old, g, 1].astype(st_r2.dtype)
        return total - float(k)

    def select(break_ties):
        worst1 = jnp.zeros((1, lanes), F32)
        worst2 = jnp.zeros((1, lanes), F32)
        for g in range(n_groups):
            worst1 = jnp.maximum(worst1, stage1(g, break_ties))
            worst2 = jnp.maximum(worst2, stage2(g, break_ties))
        ex_scr[...] = jnp.maximum(jnp.where(s < n_units, worst1, 0.0),
                                  jnp.where((s >= 1) & (s <= n_units), worst2, 0.0))

    def evaluate_experts():
        act = _gelu_tanh(_dot(u_ref[...], h2t_ref[...])).astype(BF16)
        tb = act.shape[1]
        zero = jnp.zeros((BF16_ROWS, tb), BF16)
        for il in range(i1_per_blk):
            row = pl.ds(je * i1_per_blk + il, 1)
            n1 = [jnp.broadcast_to(n1_scr[cur, h, row, :], (BF16_ROWS, tb)).astype(BF16) for h in range(nh)]
            e1 = [jnp.broadcast_to(e1_scr[cur, h, row, :], (BF16_ROWS, tb)).astype(BF16) for h in range(nh)]
            for m in range(nk // BF16_ROWS):
                rows = slice(m * BF16_ROWS, (m + 1) * BF16_ROWS)
                w = None
                for h in range(nh):
                    sel = r2_scr[cur, h, rows, :] < n1[h]
                    prod = e2_scr[cur, h, rows, :] * e1[h]
                    w = jnp.where(sel, prod, zero) if w is None else jnp.where(sel, w + prod, w)
                lo = il * nk + m * BF16_ROWS
                p_ref[lo:lo + BF16_ROWS, :] = w * act[lo:lo + BF16_ROWS, :]
        acc_ref[...] += _dot(vt_ref[...], p_ref[...])

    @pl.when(s <= nh)
    def _():
        select(False)

    @pl.when((s > nh) & (s <= n_units))
    def _():
        select(False)
        evaluate_experts()

    @pl.when(s > n_units)
    def _():
        ex_scr[...] = jnp.zeros_like(ex_scr)
        evaluate_experts()

    @pl.when(jnp.max(ex_scr[...]) > 0.0)
    def _():
        select(True)

    @pl.when(s <= n_units)
    def _():
        for r in (ss_scr, rk_scr, top_scr):
            r[old] = r[new]
        n1_scr[nxt, h2] = st_n1[...]
        e1_scr[nxt, h2] = st_e1[...]
        e2_scr[nxt, h2] = st_e2[...]
        r2_scr[nxt, h2] = st_r2[...]

    @pl.when((s >= nh) & (s % nh == 0))
    def _():
        for r in (n1_scr, e1_scr, e2_scr, r2_scr):
            r[cur] = r[nxt]

    @pl.when((s > nh) & (je == nh - 1))
    def _():
        y = x1_ref[...] + acc_ref[...].T
        ms = jnp.mean(y * y, axis=-1, keepdims=True)
        o_ref[...] = y * lax.rsqrt(ms + NORM_EPS) * fnw_ref[...]


def _peer_ffn(scores, h2t, u_bf, vt_bf, x1, fnw, tb=512):
    t, d = x1.shape
    ne = u_bf.shape[0]
    nh, nk = PEER_HEADS, PEER_NKEYS
    eb = ne // nh
    ng = tb // LANES
    n_units = (t // tb) * nh

    def unit1(s):
        u = jnp.minimum(s, n_units - 1)
        return u // nh, u % nh

    def eblk(s):
        e = jnp.clip(s - (nh + 1), 0, n_units - 1)
        return e // nh, e % nh

    return pl.pallas_call(
        functools.partial(_ffn_body, i1_per_blk=eb // nk, n_groups=ng, n_units=n_units),
        grid=(n_units + nh + 1,),
        in_specs=[
            pl.BlockSpec((None, 2, nk, tb), lambda s: (unit1(s)[1], 0, 0, unit1(s)[0])),
            pl.BlockSpec((d, tb), lambda s: (0, eblk(s)[0])),
            pl.BlockSpec((eb, d), lambda s: (eblk(s)[1], 0)),
            pl.BlockSpec((d, eb), lambda s: (0, eblk(s)[1])),
            pl.BlockSpec((tb, d), lambda s: (eblk(s)[0], 0)),
            pl.BlockSpec((1, d), lambda s: (0, 0)),
        ],
        out_specs=pl.BlockSpec((tb, d), lambda s: (eblk(s)[0], 0)),
        out_shape=jax.ShapeDtypeStruct((t, d), F32),
        scratch_shapes=[
            pltpu.VMEM((d, tb), F32),
            pltpu.VMEM((eb, tb), BF16),
            pltpu.VMEM((2, nh, nk, tb), F32),
            pltpu.VMEM((2, nh, nk, tb), F32),
            pltpu.VMEM((2, nh, nk, tb), BF16),
            pltpu.VMEM((2, nh, nk, tb), BF16),
            pltpu.VMEM((2, ng, 2, nk, LANES), F32),
            pltpu.VMEM((2, ng, 2, nk, LANES), F32),
            pltpu.VMEM((2, ng, 2, PEER_TOPK, LANES), F32),
            pltpu.VMEM((1, LANES), F32),
            pltpu.VMEM((nk, tb), F32), pltpu.VMEM((nk, tb), F32),
            pltpu.VMEM((nk, tb), BF16), pltpu.VMEM((nk, tb), BF16),
        ],
        compiler_params=_params(("arbitrary",)),
        name="peer_ffn",
    )(scores, h2t, u_bf, vt_bf, x1, fnw)


def _rope_tables(s):
    half = RET_DK // 2
    inv = ROPE_BASE ** (-jnp.arange(half, dtype=F32) / half)
    ang = jnp.arange(s, dtype=F32)[:, None] * inv[None, :]
    return jnp.cos(ang), jnp.sin(ang)


def kernel(x, norm_mix_w, w_in, hg_lower_bounds, hg_norm_w, w_branch_hg, w_branch_ret, w_out, norm_ffn_w,
           peer_w_q, peer_sub_keys, expert_u, expert_v, final_norm_w):
    b, s, d = x.shape
    t = b * s
    x2 = x.reshape(t, d)

    proj = _inproj(x2, norm_mix_w[0:1], w_in[0].astype(BF16), BF16)
    proj3 = proj.reshape(b, s, IN_WIDTH)

    cos_t, sin_t = _rope_tables(s)
    log_gamma = jnp.log(1.0 - jnp.exp2(-5.0 - jnp.arange(RET_HEADS, dtype=F32)))
    log_gamma = jnp.broadcast_to(log_gamma[:, None, None], (RET_HEADS, 1, RET_DV))
    o_hg, o_ret = _mixers(proj3, hg_lower_bounds, hg_norm_w[0:1], cos_t, sin_t, log_gamma)

    x1, h2t, scores = _merge(
        o_hg.reshape(t, -1), o_ret.reshape(t, -1), proj, x2,
        w_branch_hg[0].astype(BF16), w_branch_ret[0].astype(BF16), w_out[0].astype(BF16),
        norm_ffn_w[0:1], peer_w_q[0].astype(BF16), peer_sub_keys[0].astype(BF16))

    out = _peer_ffn(scores, h2t, expert_u[0].astype(BF16), expert_v[0].astype(BF16).T, x1, final_norm_w[None, :])
    return out.reshape(b, s, d)
```
